```python
import math
import jax, jax.numpy as jnp
from jax import lax
import numpy as np

D_MODEL = 1024
BATCH = 8
SEQ = 2048
DEPTH = 1
DEC_BATCH = 16
DEC_SEQ = 2048
PAST_LEN = 128

GRID_W = 64
NA_HEADS = 16
NA_HEAD_DIM = 64
NA_WIDTH = NA_HEADS * NA_HEAD_DIM
WIN_R = 8
WIN_C = 16
QCB = WIN_C
KCB = 2 * WIN_C
NCB = GRID_W // QCB
SSD_EXPAND = 2
D_INNER = SSD_EXPAND * D_MODEL
SSD_HEAD_DIM = 64
SSD_HEADS = D_INNER // SSD_HEAD_DIM
SSD_GROUPS = 8
SSD_STATE = 128
SSD_CONV = 5
SSD_CHUNK = 128
CONV_CH = D_INNER + 2 * SSD_GROUPS * SSD_STATE
MEM_TOKENS = 256
MEM_HEADS = 4
MEM_HEAD_DIM = 256
MEM_WIDTH = MEM_HEADS * MEM_HEAD_DIM
N_BRANCH = 3
D_FF = 2816
IN_SIZES = (NA_WIDTH, NA_WIDTH, NA_WIDTH, MEM_WIDTH, D_INNER, CONV_CH, 2 * SSD_HEADS, N_BRANCH * D_MODEL)
IN_WIDTH = 3 * NA_WIDTH + MEM_WIDTH + D_INNER + CONV_CH + 2 * SSD_HEADS + N_BRANCH * D_MODEL
RMS_EPS = 1e-6
NEG_INF = -1e30

kernel_name = 'hybrid_na_ssd_memory_encoder'


def split_sizes(x, sizes):
    idx = [int(i) for i in np.cumsum(sizes)[:-1]]
    return jnp.split(x, idx, axis=-1)


def rms_norm(x, g):
    xf = x.astype(jnp.float32)
    y = xf * lax.rsqrt(jnp.mean(xf * xf, axis=-1, keepdims=True) + RMS_EPS)
    return (y * g.astype(jnp.float32)).astype(x.dtype)


def grouped_rms_norm(x, g, groups):
    shp = x.shape
    xg = x.reshape(shp[:-1] + (groups, shp[-1] // groups))
    return rms_norm(xg, g.reshape(groups, shp[-1] // groups)).reshape(shp)


def swiglu(x, w_gate, w_up, w_down):
    return (jax.nn.silu(x @ w_gate) * (x @ w_up)) @ w_down


def depthwise_conv(x, w, b):
    y = lax.conv_general_dilated(x, w[:, None, :], window_strides=(1,),
                                 padding=[(SSD_CONV // 2, SSD_CONV // 2)],
                                 dimension_numbers=('NWC', 'WIO', 'NWC'),
                                 feature_group_count=x.shape[-1])
    return y + b


def neighborhood_attention(q, k, v, rpb):
    bsz, t, h, dh = q.shape
    rows = t // GRID_W
    kr = min(WIN_R, rows)
    q = q.reshape(bsz, rows, GRID_W, h, dh)
    k = k.reshape(bsz, rows, GRID_W, h, dh)
    v = v.reshape(bsz, rows, GRID_W, h, dh)
    key_c0 = np.clip(np.arange(NCB) * QCB - WIN_C // 2, 0, GRID_W - KCB)
    kcol = key_c0[:, None] + np.arange(KCB)[None, :]
    qcol = np.arange(NCB)[:, None] * QCB + np.arange(QCB)[None, :]
    col_start = np.clip(qcol - WIN_C // 2, 0, GRID_W - WIN_C)
    kc = kcol[:, None, :]
    col_valid = (kc >= col_start[..., None]) & (kc < col_start[..., None] + WIN_C)
    rel_col = np.clip(kc - qcol[..., None] + WIN_C - 1, 0, 2 * WIN_C - 2)
    rpb_col = rpb.astype(jnp.float32)[:, :, rel_col]
    valid = jnp.asarray(col_valid)[None, None, :, :, None, :]
    scale = dh ** -0.5

    def row_block(r):
        rs = jnp.clip(r - kr // 2, 0, rows - kr)
        q_r = lax.dynamic_index_in_dim(q, r, axis=1, keepdims=False).reshape(bsz, NCB, QCB, h, dh)
        k_b = lax.dynamic_slice_in_dim(k, rs, kr, axis=1)[:, :, kcol]
        v_b = lax.dynamic_slice_in_dim(v, rs, kr, axis=1)[:, :, kcol]
        s = jnp.einsum('bnqhd,brnkhd->bhnqrk', q_r, k_b, preferred_element_type=jnp.float32) * scale
        rel_row = rs + jnp.arange(kr) - r + (WIN_R - 1)
        bias = jnp.transpose(rpb_col[:, rel_row], (0, 2, 3, 1, 4))
        s = jnp.where(valid, s + bias[None], NEG_INF)
        p = jax.nn.softmax(s.reshape(s.shape[:4] + (kr * KCB,)), axis=-1).reshape(s.shape)
        o = jnp.einsum('bhnqrk,brnkhd->bnqhd', p.astype(v.dtype), v_b)
        return o.reshape(bsz, GRID_W, h, dh)

    out = lax.map(row_block, jnp.arange(rows))
    return jnp.transpose(out, (1, 0, 2, 3, 4)).reshape(bsz, t, h * dh)


def ssd_chunked(x, dt, a, bm, cm):
    x = x.astype(jnp.float32)
    dt = dt.astype(jnp.float32)
    bm = bm.astype(jnp.float32)
    cm = cm.astype(jnp.float32)
    bsz, t, h, p = x.shape
    g, n = bm.shape[2], bm.shape[3]
    r = h // g
    l = SSD_CHUNK
    nc = t // l
    xc = (x * dt[..., None]).reshape(bsz, nc, l, g, r, p)
    bc = bm.reshape(bsz, nc, l, g, n)
    cc = cm.reshape(bsz, nc, l, g, n)
    a_cum = jnp.cumsum((dt * a).reshape(bsz, nc, l, g, r), axis=2)
    seg = a_cum[:, :, :, None] - a_cum[:, :, None, :]
    lower = jnp.asarray(np.tril(np.ones((l, l), dtype=bool)))[:, :, None, None]
    decay = jnp.exp(jnp.where(lower, seg, -jnp.inf))
    cb = jnp.einsum('bclgn,bcsgn->bclsg', cc, bc)
    y_diag = jnp.einsum('bclsgr,bcsgrp->bclgrp', cb[..., None] * decay, xc)
    decay_s = jnp.exp(a_cum[:, :, -1:] - a_cum)
    states = jnp.einsum('bclgn,bclgrp->bcgrpn', bc, xc * decay_s[..., None])
    chunk_decay = jnp.exp(a_cum[:, :, -1])

    def step(hs, inp):
        s, d = inp
        return hs * d[..., None, None] + s, hs

    h0 = jnp.zeros((bsz, g, r, p, n), jnp.float32)
    _, prev = lax.scan(step, h0, (jnp.moveaxis(states, 1, 0), jnp.moveaxis(chunk_decay, 1, 0)))
    prev = jnp.moveaxis(prev, 0, 1)
    y_off = jnp.einsum('bclgn,bcgrpn->bclgrp', cc, prev) * jnp.exp(a_cum)[..., None]
    return (y_diag + y_off).reshape(bsz, t, h, p)


def memory_attention(q, k, v):
    bsz, t, h, dh = q.shape
    s = jnp.einsum('bthd,bmhd->bhtm', q, k, preferred_element_type=jnp.float32) * dh ** -0.5
    pr = jax.nn.softmax(s, axis=-1).astype(v.dtype)
    return jnp.einsum('bhtm,bmhd->bthd', pr, v).reshape(bsz, t, h * dh)


def encoder_layer(x, mem, ffn1_norm, ffn1_w_gate, ffn1_w_up, ffn1_w_down, mix_norm, w_in,
                  na_q_norm, na_k_norm, na_rpb, conv_w, conv_b, dt_bias_f, dt_bias_b,
                  a_log_f, a_log_b, ssd_d, ssd_norm, mem_norm, w_mem_kv, mem_q_norm, mem_k_norm,
                  w_br_na, w_br_ssd, w_br_mem, w_out, ffn2_norm, ffn2_w_gate, ffn2_w_up, ffn2_w_down):
    bsz, t, _ = x.shape
    x = x + 0.5 * swiglu(rms_norm(x, ffn1_norm), ffn1_w_gate, ffn1_w_up, ffn1_w_down)
    u = rms_norm(x, mix_norm)
    proj = u @ w_in
    q_na, k_na, v_na, q_mem, z, xbc, dt_raw, gate_logits = split_sizes(proj, IN_SIZES)
    q_na = rms_norm(q_na.reshape(bsz, t, NA_HEADS, NA_HEAD_DIM), na_q_norm)
    k_na = rms_norm(k_na.reshape(bsz, t, NA_HEADS, NA_HEAD_DIM), na_k_norm)
    v_na = v_na.reshape(bsz, t, NA_HEADS, NA_HEAD_DIM)
    o_na = neighborhood_attention(q_na, k_na, v_na, na_rpb)
    xbc = jax.nn.silu(depthwise_conv(xbc, conv_w, conv_b))
    xs, bm, cm = split_sizes(xbc, (D_INNER, SSD_GROUPS * SSD_STATE, SSD_GROUPS * SSD_STATE))
    xs = xs.reshape(bsz, t, SSD_HEADS, SSD_HEAD_DIM)
    bm = bm.reshape(bsz, t, SSD_GROUPS, SSD_STATE)
    cm = cm.reshape(bsz, t, SSD_GROUPS, SSD_STATE)
    dt_f, dt_b = split_sizes(dt_raw.astype(jnp.float32), (SSD_HEADS, SSD_HEADS))
    dt_f = jax.nn.softplus(dt_f + dt_bias_f.astype(jnp.float32))
    dt_b = jax.nn.softplus(dt_b + dt_bias_b.astype(jnp.float32))
    a_f = -jnp.exp(a_log_f.astype(jnp.float32))
    a_b = -jnp.exp(a_log_b.astype(jnp.float32))
    y_fwd = ssd_chunked(xs, dt_f, a_f, bm, cm)
    y_bwd = jnp.flip(ssd_chunked(jnp.flip(xs, 1), jnp.flip(dt_b, 1), a_b, jnp.flip(bm, 1), jnp.flip(cm, 1)), 1)
    y_ssd = y_fwd + y_bwd + ssd_d.astype(jnp.float32)[:, None] * xs.astype(jnp.float32)
    y_ssd = y_ssd.reshape(bsz, t, D_INNER).astype(x.dtype) * jax.nn.silu(z)
    o_ssd = grouped_rms_norm(y_ssd, ssd_norm, SSD_GROUPS)
    kv_m = rms_norm(mem, mem_norm) @ w_mem_kv
    k_m, v_m = split_sizes(kv_m, (MEM_WIDTH, MEM_WIDTH))
    m = mem.shape[1]
    q_m = rms_norm(q_mem.reshape(bsz, t, MEM_HEADS, MEM_HEAD_DIM), mem_q_norm)
    k_m = rms_norm(k_m.reshape(bsz, m, MEM_HEADS, MEM_HEAD_DIM), mem_k_norm)
    v_m = v_m.reshape(bsz, m, MEM_HEADS, MEM_HEAD_DIM)
    o_mem = memory_attention(q_m, k_m, v_m)
    g_na, g_ssd, g_mem = split_sizes(jax.nn.sigmoid(gate_logits), (D_MODEL, D_MODEL, D_MODEL))
    merged = g_na * (o_na @ w_br_na) + g_ssd * (o_ssd @ w_br_ssd) + g_mem * (o_mem @ w_br_mem)
    x = x + merged @ w_out
    x = x + 0.5 * swiglu(rms_norm(x, ffn2_norm), ffn2_w_gate, ffn2_w_up, ffn2_w_down)
    return x


def setup_inputs(seed: int = 0) -> dict:
    key = jax.random.key(seed)
    ks = iter(jax.random.split(key, 64))

    def nrm(shape, scale):
        return scale * jax.random.normal(next(ks), shape, jnp.float32)

    def gain(n):
        return 1.0 + nrm((DEPTH, n), 0.02)

    def dt_bias():
        dt = jnp.exp(jax.random.uniform(next(ks), (DEPTH, SSD_HEADS), jnp.float32,
                                        minval=math.log(1e-3), maxval=math.log(1e-1)))
        return dt + jnp.log(-jnp.expm1(-dt))

    def a_log():
        return jnp.log(jax.random.uniform(next(ks), (DEPTH, SSD_HEADS), jnp.float32, minval=1.0, maxval=16.0))

    return {
        'x_prompt': nrm((BATCH, SEQ, D_MODEL), 1.0),
        'x_sample': nrm((DEC_BATCH, DEC_SEQ, D_MODEL), 1.0),
        'mem_prompt': nrm((BATCH, MEM_TOKENS, D_MODEL), 1.0),
        'mem_sample': nrm((DEC_BATCH, MEM_TOKENS, D_MODEL), 1.0),
        'ffn1_norm': gain(D_MODEL),
        'ffn1_w_gate': nrm((DEPTH, D_MODEL, D_FF), D_MODEL ** -0.5),
        'ffn1_w_up': nrm((DEPTH, D_MODEL, D_FF), D_MODEL ** -0.5),
        'ffn1_w_down': nrm((DEPTH, D_FF, D_MODEL), D_FF ** -0.5),
        'mix_norm': gain(D_MODEL),
        'w_in': nrm((DEPTH, D_MODEL, IN_WIDTH), D_MODEL ** -0.5),
        'na_q_norm': gain(NA_HEAD_DIM),
        'na_k_norm': gain(NA_HEAD_DIM),
        'na_rpb': nrm((DEPTH, NA_HEADS, 2 * WIN_R - 1, 2 * WIN_C - 1), 0.02),
        'conv_w': nrm((DEPTH, SSD_CONV, CONV_CH), SSD_CONV ** -0.5),
        'conv_b': nrm((DEPTH, CONV_CH), 0.02),
        'dt_bias_f': dt_bias(),
        'dt_bias_b': dt_bias(),
        'a_log_f': a_log(),
        'a_log_b': a_log(),
        'ssd_d': 1.0 + nrm((DEPTH, SSD_HEADS), 0.02),
        'ssd_norm': gain(D_INNER),
        'mem_norm': gain(D_MODEL),
        'w_mem_kv': nrm((DEPTH, D_MODEL, 2 * MEM_WIDTH), D_MODEL ** -0.5),
        'mem_q_norm': gain(MEM_HEAD_DIM),
        'mem_k_norm': gain(MEM_HEAD_DIM),
        'w_br_na': nrm((DEPTH, NA_WIDTH, D_MODEL), NA_WIDTH ** -0.5),
        'w_br_ssd': nrm((DEPTH, D_INNER, D_MODEL), D_INNER ** -0.5),
        'w_br_mem': nrm((DEPTH, MEM_WIDTH, D_MODEL), MEM_WIDTH ** -0.5),
        'w_out': nrm((DEPTH, D_MODEL, D_MODEL), D_MODEL ** -0.5),
        'ffn2_norm': gain(D_MODEL),
        'ffn2_w_gate': nrm((DEPTH, D_MODEL, D_FF), D_MODEL ** -0.5),
        'ffn2_w_up': nrm((DEPTH, D_MODEL, D_FF), D_MODEL ** -0.5),
        'ffn2_w_down': nrm((DEPTH, D_FF, D_MODEL), D_FF ** -0.5),
    }


def reference(x_prompt, x_sample, mem_prompt, mem_sample, ffn1_norm, ffn1_w_gate, ffn1_w_up,
              ffn1_w_down, mix_norm, w_in, na_q_norm, na_k_norm, na_rpb, conv_w, conv_b,
              dt_bias_f, dt_bias_b, a_log_f, a_log_b, ssd_d, ssd_norm, mem_norm, w_mem_kv,
              mem_q_norm, mem_k_norm, w_br_na, w_br_ssd, w_br_mem, w_out, ffn2_norm,
              ffn2_w_gate, ffn2_w_up, ffn2_w_down):
    layer_weights = (ffn1_norm, ffn1_w_gate, ffn1_w_up, ffn1_w_down, mix_norm, w_in,
                     na_q_norm, na_k_norm, na_rpb, conv_w, conv_b, dt_bias_f, dt_bias_b,
                     a_log_f, a_log_b, ssd_d, ssd_norm, mem_norm, w_mem_kv, mem_q_norm,
                     mem_k_norm, w_br_na, w_br_ssd, w_br_mem, w_out, ffn2_norm, ffn2_w_gate,
                     ffn2_w_up, ffn2_w_down)
    y_prompt = x_prompt
    y_sample = x_sample
    for l in range(DEPTH):
        w_l = [w[l] for w in layer_weights]
        y_prompt = encoder_layer(y_prompt, mem_prompt, *w_l)
        y_sample = encoder_layer(y_sample, mem_sample, *w_l)
    return (y_prompt, y_sample)
```

```python
import functools

import jax
import jax.numpy as jnp
import numpy as np
from jax import lax
from jax.experimental import pallas as pl
from jax.experimental.pallas import tpu as pltpu

F32 = jnp.float32
BF16 = jnp.bfloat16

D_MODEL = 1024
GRID_W = 64
NA_HEADS = 16
NA_HEAD_DIM = 64
WIN_R = 8
WIN_C = 16
D_INNER = 2048
SSD_HEADS = 32
SSD_HEAD_DIM = 64
SSD_GROUPS = 8
SSD_STATE = 128
SSD_CONV = 5
CHUNK = 128
MEM_TOKENS = 256
MEM_HEADS = 4
MEM_HEAD_DIM = 256
D_FF = 2816
RMS_EPS = 1e-6
NEG_INF = -1e30

LANES = 128
HEADS_PER_GROUP = SSD_HEADS // SSD_GROUPS
GROUP_W = HEADS_PER_GROUP * SSD_HEAD_DIM
PROJ_TILE = 1024
T_Q, T_K, T_V, T_QM, T_Z, T_XS, T_BM, T_CM, T_GATE = 0, 1, 2, 3, 4, 6, 8, 9, 10
N_PROJ_TILES = 13
VMEM_LIMIT = 56 * 1024 * 1024

NT_DIMS = (((1,), (1,)), ((), ()))


def _params(sem, vmem=VMEM_LIMIT):
    return pltpu.CompilerParams(dimension_semantics=sem, vmem_limit_bytes=vmem)


def _const_spec(shape):
    nd = len(shape)
    return pl.BlockSpec(shape, lambda *_: (0,) * nd, pipeline_mode=pl.Buffered(1))


def _sigmoid(x):
    return 1.0 / (1.0 + jnp.exp(-x))


def _silu(x):
    return x * _sigmoid(x)


def _rms(x, g):
    return x * lax.rsqrt(jnp.mean(x * x, axis=-1, keepdims=True) + RMS_EPS) * g


def _ffn_kernel(x_ref, g_ref, wg_ref, wu_ref, wd_ref, g2_ref, o_ref, *u_ref):
    x = x_ref[...]
    xn = _rms(x, g_ref[...]).astype(BF16)
    gate = jnp.dot(xn, wg_ref[...], preferred_element_type=F32)
    up = jnp.dot(xn, wu_ref[...], preferred_element_type=F32)
    h = (_silu(gate) * up).astype(BF16)
    y = x + 0.5 * jnp.dot(h, wd_ref[...], preferred_element_type=F32)
    o_ref[...] = y
    if u_ref:
        u_ref[0][...] = _rms(y, g2_ref[...]).astype(BF16)


def _ffn(x, g, wg, wu, wd, g2, with_u, tm=512):
    n = x.shape[0]
    tok = pl.BlockSpec((tm, D_MODEL), lambda i: (i, 0))
    out_shape = [jax.ShapeDtypeStruct((n, D_MODEL), F32)]
    out_specs = [tok]
    if with_u:
        out_shape.append(jax.ShapeDtypeStruct((n, D_MODEL), BF16))
        out_specs.append(tok)
    res = pl.pallas_call(
        _ffn_kernel,
        grid=(n // tm,),
        in_specs=[tok, _const_spec((1, D_MODEL)), _const_spec((D_MODEL, D_FF)),
                  _const_spec((D_MODEL, D_FF)), _const_spec((D_FF, D_MODEL)), _const_spec((1, D_MODEL))],
        out_specs=out_specs,
        out_shape=out_shape,
        compiler_params=_params(("parallel",)),
        name="ffn_u" if with_u else "ffn",
    )(x, g, wg, wu, wd, g2)
    return res


def _inproj_kernel(u_ref, w_ref, gain_ref, nmat_ref, wdc_ref, wdr_ref, o_ref, dtc_ref, dtr_ref):
    j = pl.program_id(1)
    u = u_ref[...]
    acc = jnp.dot(u, w_ref[...], preferred_element_type=F32)

    @pl.when(j == 0)
    def _():
        dtc_ref[...] = jnp.dot(u, wdc_ref[...], preferred_element_type=F32)
        dtr_ref[...] = lax.dot_general(wdr_ref[...], u, NT_DIMS, preferred_element_type=F32)

    is_norm = (j == T_Q) | (j == T_K) | (j == T_QM)

    @pl.when(is_norm)
    def _():
        nm = nmat_ref[0]
        for c in range(PROJ_TILE // 256):
            a = acc[:, c * 256:(c + 1) * 256]
            ms = jnp.dot((a * a).astype(BF16), nm, preferred_element_type=F32)
            g = gain_ref[0, :, c * 256:(c + 1) * 256]
            o_ref[:, c * 256:(c + 1) * 256] = (a * lax.rsqrt(ms + RMS_EPS) * g).astype(BF16)

    is_silu = (j == T_Z) | (j == T_Z + 1)

    @pl.when(is_silu)
    def _():
        o_ref[...] = _silu(acc).astype(BF16)

    @pl.when(j >= T_GATE)
    def _():
        o_ref[...] = _sigmoid(acc).astype(BF16)

    @pl.when(jnp.logical_not(is_norm | is_silu | (j >= T_GATE)))
    def _():
        o_ref[...] = acc.astype(BF16)


def _inproj(u, w_main, gains, nmats, wdt_c, wdt_r, tm=1024):
    n = u.shape[0]
    return pl.pallas_call(
        _inproj_kernel,
        grid=(n // tm, N_PROJ_TILES),
        in_specs=[
            pl.BlockSpec((tm, D_MODEL), lambda i, j: (i, 0)),
            pl.BlockSpec((D_MODEL, PROJ_TILE), lambda i, j: (0, j)),
            pl.BlockSpec((1, 1, PROJ_TILE), lambda i, j: (jnp.minimum(j, T_QM), 0, 0)),
            pl.BlockSpec((1, 256, 256), lambda i, j: (jnp.where(j == T_QM, 1, 0), 0, 0)),
            pl.BlockSpec((D_MODEL, LANES), lambda i, j: (0, 0)),
            pl.BlockSpec((LANES, D_MODEL), lambda i, j: (0, 0)),
        ],
        out_specs=[
            pl.BlockSpec((tm, PROJ_TILE), lambda i, j: (i, j)),
            pl.BlockSpec((tm, LANES), lambda i, j: (i, 0)),
            pl.BlockSpec((LANES, tm), lambda i, j: (0, i)),
        ],
        out_shape=[
            jax.ShapeDtypeStruct((n, N_PROJ_TILES * PROJ_TILE), BF16),
            jax.ShapeDtypeStruct((n, LANES), F32),
            jax.ShapeDtypeStruct((LANES, n), F32),
        ],
        compiler_params=_params(("parallel", "arbitrary")),
        name="in_proj",
    )(u, w_main, gains, nmats, wdt_c, wdt_r)


def _softplus(x):
    return jnp.maximum(x, 0.0) + jnp.log1p(jnp.exp(-jnp.abs(x)))


def _split3(x):
    p1 = x.astype(BF16)
    r1 = x - p1.astype(F32)
    p2 = r1.astype(BF16)
    p3 = (r1 - p2.astype(F32)).astype(BF16)
    return p1, p2, p3


def _dtprep_kernel(dtc_ref, dtr_ref, bc_ref, ac_ref, fc_ref, br_ref, ar_ref, fr_ref,
                   cola_ref, coldt_ref, colw_ref, rowa_ref):
    t = dtc_ref.shape[1]
    li = lax.broadcasted_iota(jnp.int32, (CHUNK, CHUNK), 0)
    ui = lax.broadcasted_iota(jnp.int32, (CHUNK, CHUNK), 1)
    tri_le = (ui <= li).astype(F32).astype(BF16)
    tri_gt = (ui > li).astype(F32).astype(BF16)
    tri_ge = (ui >= li).astype(F32).astype(BF16)
    isf_c = fc_ref[...] > 0.5
    isf_r = fr_ref[...] > 0.5
    for c in range(t // CHUNK):
        sl = slice(c * CHUNK, (c + 1) * CHUNK)
        dt = _softplus(dtc_ref[0, sl, :] + bc_ref[...])
        da = dt * ac_ref[...]
        pre = jnp.zeros((CHUNK, LANES), F32)
        suf = jnp.zeros((CHUNK, LANES), F32)
        for p in _split3(da):
            pre += jnp.dot(tri_le, p, preferred_element_type=F32)
            suf += jnp.dot(tri_gt, p, preferred_element_type=F32)
        cola_ref[0, sl, :] = jnp.where(isf_c, pre, suf + da)
        coldt_ref[0, sl, :] = dt
        colw_ref[0, sl, :] = jnp.exp(jnp.where(isf_c, suf, pre - da)) * dt
        dtr = _softplus(dtr_ref[:, sl] + br_ref[...])
        dar = dtr * ar_ref[...]
        pre_r = jnp.zeros((LANES, CHUNK), F32)
        suf_r = jnp.zeros((LANES, CHUNK), F32)
        for p in _split3(dar):
            pre_r += jnp.dot(p, tri_ge, preferred_element_type=F32)
            suf_r += lax.dot_general(p, tri_gt, NT_DIMS, preferred_element_type=F32)
        rowa_ref[0, :, sl] = jnp.where(isf_r, pre_r, suf_r + dar)


def _dtprep(dtc, dtr, bias_c, a_c, isf_c, b, t):
    colspec = pl.BlockSpec((1, t, LANES), lambda i: (i, 0, 0))
    vc = pl.BlockSpec((1, LANES), lambda i: (0, 0))
    vr = pl.BlockSpec((LANES, 1), lambda i: (0, 0))
    return pl.pallas_call(
        _dtprep_kernel,
        grid=(b,),
        in_specs=[colspec, pl.BlockSpec((LANES, t), lambda i: (0, i)), vc, vc, vc, vr, vr, vr],
        out_specs=[colspec, colspec, colspec, pl.BlockSpec((1, LANES, t), lambda i: (i, 0, 0))],
        out_shape=[jax.ShapeDtypeStruct((b, t, LANES), F32)] * 3 + [jax.ShapeDtypeStruct((b, LANES, t), F32)],
        compiler_params=_params(("parallel",)),
        name="dt_prep",
    )(dtc.reshape(b, t, LANES), dtr, bias_c, a_c, isf_c, bias_c.reshape(LANES, 1), a_c.reshape(LANES, 1),
      isf_c.reshape(LANES, 1))


NA_ROWS_PER_STEP = 8
NA_KEYS = WIN_R * GRID_W


def _na_kernel(q_ref, k_ref, v_ref, bias_ref, o_ref):
    rows = k_ref.shape[1] // GRID_W
    rb = pl.program_id(1)
    lane = lax.broadcasted_iota(jnp.int32, (GRID_W, LANES), 1)
    lo = lane < NA_HEAD_DIM
    lane1 = lax.broadcasted_iota(jnp.int32, (1, LANES), 1)
    mask_a = (lane1 < NA_HEAD_DIM).astype(F32).astype(BF16)
    mask_b = (lane1 >= NA_HEAD_DIM).astype(F32).astype(BF16)

    def row_body(rl, carry):
        r = rb * NA_ROWS_PER_STEP + rl
        rs = jnp.clip(r - WIN_R // 2, 0, rows - WIN_R)
        s0 = (WIN_R - 1) - (r - rs)
        tbl = (s0 % 2) * 8 + s0 // 2
        q_off = pl.multiple_of(rl * GRID_W, GRID_W)
        k_off = pl.multiple_of(rs * GRID_W, GRID_W)
        for p in range(NA_HEADS // 2):
            cs = slice(p * LANES, (p + 1) * LANES)
            q2 = q_ref[0, pl.ds(q_off, GRID_W), cs]
            qblk = jnp.concatenate([q2 * mask_a, q2 * mask_b], axis=0)
            k2 = k_ref[0, pl.ds(k_off, NA_KEYS), cs]
            v2 = v_ref[0, pl.ds(k_off, NA_KEYS), cs]
            s = lax.dot_general(qblk, k2, NT_DIMS, preferred_element_type=F32)
            bias = jnp.concatenate([bias_ref[p, tbl + i] for i in range(NA_KEYS // LANES)], axis=1)
            s = s + bias
            m = jnp.max(s, axis=1, keepdims=True)
            e = jnp.exp(s - m)
            l = jnp.sum(e, axis=1, keepdims=True)
            o2 = jnp.dot(e.astype(BF16), v2, preferred_element_type=F32) * (1.0 / l)
            o = jnp.where(lo, o2[:GRID_W], o2[GRID_W:])
            o_ref[0, pl.ds(q_off, GRID_W), cs] = o.astype(BF16)
        return carry

    lax.fori_loop(0, NA_ROWS_PER_STEP, row_body, 0)


def _na(proj, bias_tbl, b, t):
    blk = NA_ROWS_PER_STEP * GRID_W
    return pl.pallas_call(
        _na_kernel,
        grid=(b, t // blk),
        in_specs=[
            pl.BlockSpec((1, blk, PROJ_TILE), lambda i, r: (i, r, T_Q)),
            pl.BlockSpec((1, t, PROJ_TILE), lambda i, r: (i, 0, T_K)),
            pl.BlockSpec((1, t, PROJ_TILE), lambda i, r: (i, 0, T_V)),
            _const_spec(bias_tbl.shape),
        ],
        out_specs=pl.BlockSpec((1, blk, PROJ_TILE), lambda i, r: (i, r, 0)),
        out_shape=jax.ShapeDtypeStruct((b, t, NA_HEADS * NA_HEAD_DIM), BF16),
        compiler_params=_params(("parallel", "arbitrary")),
        name="na_attn",
    )(proj, proj, proj, bias_tbl)


def _na_bias_table(rpb):
    qc = np.arange(GRID_W)[:, None]
    kc = np.arange(GRID_W)[None, :]
    cs = np.clip(qc - WIN_C // 2, 0, GRID_W - WIN_C)
    valid = (kc >= cs) & (kc < cs + WIN_C)
    rel_col = np.clip(kc - qc + WIN_C - 1, 0, 2 * WIN_C - 2)
    t1 = rpb.astype(F32)[:, :, rel_col]
    t1 = jnp.where(jnp.asarray(valid)[None, None], t1, NEG_INF)
    t1 = jnp.concatenate([t1, jnp.full((NA_HEADS, 3, GRID_W, GRID_W), NEG_INF, F32)], axis=1)
    par = np.arange(2)[:, None, None]
    m = np.arange(8)[None, :, None]
    jj = np.arange(2)[None, None, :]
    rr = 2 * m + par + jj
    t2 = t1[:, rr]
    t2 = t2.reshape(NA_HEADS // 2, 2, 2, 8, 2, GRID_W, GRID_W)
    t2 = jnp.transpose(t2, (0, 2, 3, 1, 5, 4, 6))
    return t2.reshape(NA_HEADS // 2, 16, 2 * GRID_W, 2 * GRID_W)


COL_A, COL_DT, COL_W = 0, 8, 16
COL_PACK = 32


def _ssd_kernel(xs_ref, bm_ref, cm_ref, z_ref, cwx_ref, cbx_ref, cwb_ref, cbb_ref, cwc_ref, cbc_ref,
                col_ref, row_ref, aend_ref, d_ref, nrm_ref, o_ref,
                pad_s, xs_s, bm_s, cm_s, bmt_s, y_s):
    t = xs_ref.shape[1]
    nc = t // CHUNK
    n = SSD_STATE
    half = SSD_CONV // 2

    def conv_silu(src_ref, w_ref, b_ref):
        width = src_ref.shape[2]
        pad_s[0:8, :width] = jnp.zeros((8, width), F32)
        pad_s[8 + t:16 + t, :width] = jnp.zeros((8, width), F32)
        pad_s[8:8 + t, :width] = src_ref[0].astype(F32)
        acc = jnp.zeros((t, width), F32) + b_ref[...]
        for k in range(SSD_CONV):
            acc = acc + w_ref[k:k + 1, :] * pad_s[8 - half + k:8 - half + k + t, :width]
        return _silu(acc)

    xs = conv_silu(xs_ref, cwx_ref, cbx_ref)
    y_s[...] = (xs * d_ref[...]).reshape(nc, CHUNK, GROUP_W)
    xs_s[...] = xs.reshape(nc, CHUNK, GROUP_W)
    bm_s[...] = conv_silu(bm_ref, cwb_ref, cbb_ref).astype(BF16).reshape(nc, CHUNK, n)
    cm_s[...] = conv_silu(cm_ref, cwc_ref, cbc_ref).astype(BF16).reshape(nc, CHUNK, n)

    ri = lax.broadcasted_iota(jnp.int32, (n, n), 0)
    ci = lax.broadcasted_iota(jnp.int32, (n, n), 1)
    eye = (ri == ci).astype(F32).astype(BF16)
    for c in range(nc):
        bmt_s[c] = lax.dot_general(eye, bm_s[c], NT_DIMS, preferred_element_type=F32).astype(BF16)

    lower = ci <= ri
    upper = ci >= ri
    lane = lax.broadcasted_iota(jnp.int32, (CHUNK, LANES), 1)
    lo = lane < SSD_HEAD_DIM
    lane1 = lax.broadcasted_iota(jnp.int32, (1, LANES), 1)
    mask_a = (lane1 < SSD_HEAD_DIM).astype(F32).astype(BF16)
    mask_b = (lane1 >= SSD_HEAD_DIM).astype(F32).astype(BF16)
    lane_g = lax.broadcasted_iota(jnp.int32, (CHUNK, GROUP_W), 1) // SSD_HEAD_DIM

    def make_body(direction):
        mask = lower if direction == 0 else upper
        off = direction * HEADS_PER_GROUP

        def body(it, h):
            c = it if direction == 0 else nc - 1 - it
            cc = cm_s[c]
            bc_t = bmt_s[c]
            g = lax.dot_general(cc, bm_s[c], NT_DIMS, preferred_element_type=F32)
            hb = h.astype(BF16)
            xsc = xs_s[c]
            cols = col_ref[0, 0, c]
            rows = row_ref[0, 0, c]
            ccf = cc.astype(F32)
            for p in range(HEADS_PER_GROUP // 2):
                lhs = []
                scaled_c = []
                for hh in (2 * p, 2 * p + 1):
                    a_col = cols[:, COL_A + off + hh:COL_A + off + hh + 1]
                    a_row = rows[off + hh:off + hh + 1, :]
                    decay = jnp.exp(jnp.where(mask, a_col - a_row, -jnp.inf))
                    lhs.append((g * decay).astype(BF16))
                    scaled_c.append((ccf * jnp.exp(a_col)).astype(BF16))
                lhs = jnp.concatenate(lhs + scaled_c, axis=1)
                dt_a = cols[:, COL_DT + off + 2 * p:COL_DT + off + 2 * p + 1]
                dt_b = cols[:, COL_DT + off + 2 * p + 1:COL_DT + off + 2 * p + 2]
                xdt = xsc[:, p * LANES:(p + 1) * LANES] * jnp.where(lo, dt_a, dt_b)
                hp = hb[:, p * LANES:(p + 1) * LANES]
                rhs = jnp.concatenate([jnp.where(lo, xdt, 0.0).astype(BF16), jnp.where(lo, 0.0, xdt).astype(BF16),
                                       hp * mask_a, hp * mask_b], axis=0)
                yp = jnp.dot(lhs, rhs, preferred_element_type=F32)
                y_s[c, :, p * LANES:(p + 1) * LANES] += yp
            w = jnp.zeros((CHUNK, GROUP_W), F32)
            for hh in range(HEADS_PER_GROUP):
                w = jnp.where(lane_g == hh, cols[:, COL_W + off + hh:COL_W + off + hh + 1], w)
            xw = (xsc * w).astype(BF16)
            h_new = h * jnp.exp(aend_ref[0, 0, c, direction:direction + 1, :]) + jnp.dot(
                bc_t, xw, preferred_element_type=F32)
            return h_new

        return body

    h0 = jnp.zeros((n, GROUP_W), F32)
    lax.fori_loop(0, nc, make_body(0), h0)
    lax.fori_loop(0, nc, make_body(1), h0)

    y = y_s[...].reshape(t, GROUP_W) * z_ref[0].astype(F32)
    o_ref[0] = _rms(y, nrm_ref[...]).astype(BF16)


def _ssd(proj, conv_w, conv_b, colpack, rowpack, aend, d_rep, ssd_norm, b, t):
    nc = t // CHUNK
    xs_tile0 = T_XS * PROJ_TILE // GROUP_W
    z_tile0 = T_Z * PROJ_TILE // GROUP_W
    bm_tile0 = T_BM * PROJ_TILE // SSD_STATE
    cm_tile0 = T_CM * PROJ_TILE // SSD_STATE
    cb0 = D_INNER // SSD_STATE
    cc0 = cb0 + SSD_GROUPS
    return pl.pallas_call(
        _ssd_kernel,
        grid=(b, SSD_GROUPS),
        in_specs=[
            pl.BlockSpec((1, t, GROUP_W), lambda i, g: (i, 0, xs_tile0 + g)),
            pl.BlockSpec((1, t, SSD_STATE), lambda i, g: (i, 0, bm_tile0 + g)),
            pl.BlockSpec((1, t, SSD_STATE), lambda i, g: (i, 0, cm_tile0 + g)),
            pl.BlockSpec((1, t, GROUP_W), lambda i, g: (i, 0, z_tile0 + g)),
            pl.BlockSpec((SSD_CONV, GROUP_W), lambda i, g: (0, g)),
            pl.BlockSpec((1, GROUP_W), lambda i, g: (0, g)),
            pl.BlockSpec((SSD_CONV, SSD_STATE), lambda i, g: (0, cb0 + g)),
            pl.BlockSpec((1, SSD_STATE), lambda i, g: (0, cb0 + g)),
            pl.BlockSpec((SSD_CONV, SSD_STATE), lambda i, g: (0, cc0 + g)),
            pl.BlockSpec((1, SSD_STATE), lambda i, g: (0, cc0 + g)),
            pl.BlockSpec((1, 1, nc, CHUNK, COL_PACK), lambda i, g: (i, g, 0, 0, 0)),
            pl.BlockSpec((1, 1, nc, 2 * HEADS_PER_GROUP, CHUNK), lambda i, g: (i, g, 0, 0, 0)),
            pl.BlockSpec((1, 1, nc, 2, GROUP_W), lambda i, g: (i, g, 0, 0, 0)),
            pl.BlockSpec((1, GROUP_W), lambda i, g: (0, g)),
            pl.BlockSpec((1, GROUP_W), lambda i, g: (0, g)),
        ],
        out_specs=pl.BlockSpec((1, t, GROUP_W), lambda i, g: (i, 0, g)),
        out_shape=jax.ShapeDtypeStruct((b, t, D_INNER), BF16),
        scratch_shapes=[
            pltpu.VMEM((t + 16, GROUP_W), F32),
            pltpu.VMEM((nc, CHUNK, GROUP_W), F32),
            pltpu.VMEM((nc, CHUNK, SSD_STATE), BF16),
            pltpu.VMEM((nc, CHUNK, SSD_STATE), BF16),
            pltpu.VMEM((nc, SSD_STATE, CHUNK), BF16),
            pltpu.VMEM((nc, CHUNK, GROUP_W), F32),
        ],
        compiler_params=_params(("parallel", "arbitrary")),
        name="ssd",
    )(proj, proj, proj, proj, conv_w, conv_b, conv_w, conv_b, conv_w, conv_b,
      colpack, rowpack, aend, d_rep, ssd_norm)


def _mem_kernel(q_ref, mem_ref, gm_ref, wkv_ref, gk_ref, o_ref, k_s, v_s):
    width = MEM_HEADS * MEM_HEAD_DIM

    @pl.when(pl.program_id(1) == 0)
    def _():
        mn = _rms(mem_ref[0], gm_ref[...]).astype(BF16)
        kv = jnp.dot(mn, wkv_ref[...], preferred_element_type=F32)
        for h in range(MEM_HEADS):
            hs = slice(h * MEM_HEAD_DIM, (h + 1) * MEM_HEAD_DIM)
            k_s[:, hs] = _rms(kv[:, hs], gk_ref[...]).astype(BF16)
        v_s[...] = kv[:, width:].astype(BF16)

    for h in range(MEM_HEADS):
        hs = slice(h * MEM_HEAD_DIM, (h + 1) * MEM_HEAD_DIM)
        s = lax.dot_general(q_ref[0, :, hs], k_s[:, hs], NT_DIMS, preferred_element_type=F32)
        m = jnp.max(s, axis=1, keepdims=True)
        e = jnp.exp(s - m)
        l = jnp.sum(e, axis=1, keepdims=True)
        o = jnp.dot(e.astype(BF16), v_s[:, hs], preferred_element_type=F32) * (1.0 / l)
        o_ref[0, :, hs] = o.astype(BF16)


def _mem(proj, mem, g_mem, w_kv, g_k, b, t, tq=512):
    width = MEM_HEADS * MEM_HEAD_DIM
    return pl.pallas_call(
        _mem_kernel,
        grid=(b, t // tq),
        in_specs=[
            pl.BlockSpec((1, tq, PROJ_TILE), lambda i, j: (i, j, T_QM)),
            pl.BlockSpec((1, MEM_TOKENS, D_MODEL), lambda i, j: (i, 0, 0)),
            _const_spec((1, D_MODEL)),
            _const_spec((D_MODEL, 2 * width)),
            _const_spec((1, MEM_HEAD_DIM)),
        ],
        out_specs=pl.BlockSpec((1, tq, width), lambda i, j: (i, j, 0)),
        out_shape=jax.ShapeDtypeStruct((b, t, width), BF16),
        scratch_shapes=[pltpu.VMEM((MEM_TOKENS, width), BF16), pltpu.VMEM((MEM_TOKENS, width), BF16)],
        compiler_params=_params(("parallel", "arbitrary")),
        name="mem_attn",
    )(proj, mem, g_mem, w_kv, g_k)


def _merge_kernel(x_ref, ona_ref, ossd_ref, omem_ref, gna_ref, gssd_ref, gmem_ref,
                  wna_ref, wssd_ref, wmem_ref, wout_ref, o_ref):
    merged = gna_ref[...].astype(F32) * jnp.dot(ona_ref[...], wna_ref[...], preferred_element_type=F32)
    merged += gssd_ref[...].astype(F32) * jnp.dot(ossd_ref[...], wssd_ref[...], preferred_element_type=F32)
    merged += gmem_ref[...].astype(F32) * jnp.dot(omem_ref[...], wmem_ref[...], preferred_element_type=F32)
    o_ref[...] = x_ref[...] + jnp.dot(merged.astype(BF16), wout_ref[...], preferred_element_type=F32)


def _merge(x1, o_na, o_ssd, o_mem, proj, w_na, w_ssd, w_mem, w_out, tm=512):
    n = x1.shape[0]
    tok = lambda w: pl.BlockSpec((tm, w), lambda i: (i, 0))
    gate = lambda k: pl.BlockSpec((tm, PROJ_TILE), lambda i: (i, T_GATE + k))
    return pl.pallas_call(
        _merge_kernel,
        grid=(n // tm,),
        in_specs=[tok(D_MODEL), tok(D_MODEL), tok(D_INNER), tok(D_MODEL), gate(0), gate(1), gate(2),
                  _const_spec((D_MODEL, D_MODEL)), _const_spec((D_INNER, D_MODEL)),
                  _const_spec((D_MODEL, D_MODEL)), _const_spec((D_MODEL, D_MODEL))],
        out_specs=tok(D_MODEL),
        out_shape=jax.ShapeDtypeStruct((n, D_MODEL), F32),
        compiler_params=_params(("parallel",)),
        name="merge_out",
    )(x1, o_na, o_ssd, o_mem, proj, proj, proj, w_na, w_ssd, w_mem, w_out)


def _prepare_weights(ffn1_norm, ffn1_w_gate, ffn1_w_up, ffn1_w_down, mix_norm, w_in, na_q_norm, na_k_norm,
                     na_rpb, conv_w, conv_b, dt_bias_f, dt_bias_b, a_log_f, a_log_b, ssd_d, ssd_norm,
                     mem_norm, w_mem_kv, mem_q_norm, mem_k_norm, w_br_na, w_br_ssd, w_br_mem, w_out,
                     ffn2_norm, ffn2_w_gate, ffn2_w_up, ffn2_w_down):
    bf = lambda w: w.astype(BF16)
    row = lambda v: v.astype(F32).reshape(1, -1)
    dt0 = 3 * NA_HEADS * NA_HEAD_DIM + MEM_HEADS * MEM_HEAD_DIM + D_INNER + D_INNER + 2 * SSD_GROUPS * SSD_STATE
    n_dt = 2 * SSD_HEADS
    w_main = bf(jnp.concatenate([w_in[:, :dt0], w_in[:, dt0 + n_dt:]], axis=1))
    perm = np.concatenate([np.concatenate([np.arange(g * HEADS_PER_GROUP, (g + 1) * HEADS_PER_GROUP),
                                           SSD_HEADS + np.arange(g * HEADS_PER_GROUP, (g + 1) * HEADS_PER_GROUP)])
                           for g in range(SSD_GROUPS)])
    w_dt = w_in[:, dt0:dt0 + n_dt][:, perm]
    wdt_c = bf(jnp.pad(w_dt, ((0, 0), (0, LANES - n_dt))))
    pad_v = lambda v: jnp.pad(v.astype(F32)[perm], (0, LANES - n_dt)).reshape(1, LANES)
    dt_bias = pad_v(jnp.concatenate([dt_bias_f, dt_bias_b]))
    a_neg = pad_v(jnp.concatenate([-jnp.exp(a_log_f.astype(F32)), -jnp.exp(a_log_b.astype(F32))]))
    is_fwd = pad_v(jnp.concatenate([jnp.ones((SSD_HEADS,), F32), jnp.zeros((SSD_HEADS,), F32)]))
    gains = jnp.stack([
        jnp.tile(na_q_norm.astype(F32), NA_HEADS) * NA_HEAD_DIM ** -0.5,
        jnp.tile(na_k_norm.astype(F32), NA_HEADS),
        jnp.ones((PROJ_TILE,), F32),
        jnp.tile(mem_q_norm.astype(F32), MEM_HEADS) * MEM_HEAD_DIM ** -0.5,
    ]).reshape(4, 1, PROJ_TILE)
    blk = np.arange(256) // NA_HEAD_DIM
    nmat64 = (blk[:, None] == blk[None, :]).astype(np.float32) / NA_HEAD_DIM
    nmat256 = np.full((256, 256), 1.0 / MEM_HEAD_DIM, np.float32)
    nmats = jnp.asarray(np.stack([nmat64, nmat256]), BF16)
    return dict(
        ffn1=(row(ffn1_norm), bf(ffn1_w_gate), bf(ffn1_w_up), bf(ffn1_w_down), row(mix_norm)),
        ffn2=(row(ffn2_norm), bf(ffn2_w_gate), bf(ffn2_w_up), bf(ffn2_w_down), row(ffn2_norm)),
        inproj=(w_main, gains, nmats, wdt_c, wdt_c.T),
        dt=(dt_bias, a_neg, is_fwd),
        na_bias=_na_bias_table(na_rpb),
        conv=(conv_w.astype(F32), row(conv_b)),
        d_rep=jnp.repeat(ssd_d.astype(F32), SSD_HEAD_DIM).reshape(1, D_INNER),
        ssd_norm=row(ssd_norm),
        mem=(row(mem_norm), bf(w_mem_kv), row(mem_k_norm)),
        merge=(bf(w_br_na), bf(w_br_ssd), bf(w_br_mem), bf(w_out)),
    )


def _pack_decay(cola, coldt, colw, rowa, b, t):
    nc = t // CHUNK
    nd = 2 * SSD_HEADS
    per_g = 2 * HEADS_PER_GROUP
    split = lambda a: a[:, :, :nd].reshape(b, t, SSD_GROUPS, per_g)
    colpack = jnp.concatenate([split(cola), split(coldt), split(colw),
                               jnp.zeros((b, t, SSD_GROUPS, COL_PACK - 3 * per_g), F32)], axis=-1)
    colpack = jnp.transpose(colpack, (0, 2, 1, 3)).reshape(b, SSD_GROUPS, nc, CHUNK, COL_PACK)
    rowpack = rowa[:, :nd].reshape(b, SSD_GROUPS, per_g, nc, CHUNK)
    rowpack = jnp.transpose(rowpack, (0, 1, 3, 2, 4))
    ca = cola.reshape(b, nc, CHUNK, LANES)
    end_f = ca[:, :, CHUNK - 1, :nd].reshape(b, nc, SSD_GROUPS, per_g)[..., :HEADS_PER_GROUP]
    end_b = ca[:, :, 0, :nd].reshape(b, nc, SSD_GROUPS, per_g)[..., HEADS_PER_GROUP:]
    aend = jnp.stack([jnp.repeat(end_f, SSD_HEAD_DIM, axis=-1), jnp.repeat(end_b, SSD_HEAD_DIM, axis=-1)], axis=3)
    aend = jnp.transpose(aend, (0, 2, 1, 3, 4))
    return colpack, rowpack, aend


def _encoder_layer(x, mem, w):
    b, t, _ = x.shape
    n = b * t
    x1, u = _ffn(x.reshape(n, D_MODEL), *w["ffn1"], with_u=True)
    proj, dtc, dtr = _inproj(u, *w["inproj"])
    cola, coldt, colw, rowa = _dtprep(dtc, dtr, *w["dt"], b, t)
    colpack, rowpack, aend = _pack_decay(cola, coldt, colw, rowa, b, t)
    proj3 = proj.reshape(b, t, N_PROJ_TILES * PROJ_TILE)
    o_na = _na(proj3, w["na_bias"], b, t)
    o_ssd = _ssd(proj3, *w["conv"], colpack, rowpack, aend, w["d_rep"], w["ssd_norm"], b, t)
    o_mem = _mem(proj3, mem, *w["mem"], b, t)
    x2 = _merge(x1, o_na.reshape(n, -1), o_ssd.reshape(n, -1), o_mem.reshape(n, -1), proj, *w["merge"])
    (y,) = _ffn(x2, *w["ffn2"], with_u=False)
    return y.reshape(b, t, D_MODEL)


def kernel(x_prompt, x_sample, mem_prompt, mem_sample, ffn1_norm, ffn1_w_gate, ffn1_w_up, ffn1_w_down, mix_norm, w_in, na_q_norm, na_k_norm, na_rpb, conv_w, conv_b, dt_bias_f, dt_bias_b, a_log_f, a_log_b, ssd_d, ssd_norm, mem_norm, w_mem_kv, mem_q_norm, mem_k_norm, w_br_na, w_br_ssd, w_br_mem, w_out, ffn2_norm, ffn2_w_gate, ffn2_w_up, ffn2_w_down):
    layer = (ffn1_norm, ffn1_w_gate, ffn1_w_up, ffn1_w_down, mix_norm, w_in, na_q_norm, na_k_norm, na_rpb,
             conv_w, conv_b, dt_bias_f, dt_bias_b, a_log_f, a_log_b, ssd_d, ssd_norm, mem_norm, w_mem_kv,
             mem_q_norm, mem_k_norm, w_br_na, w_br_ssd, w_br_mem, w_out, ffn2_norm, ffn2_w_gate, ffn2_w_up,
             ffn2_w_down)
    assert all(p.shape[0] == 1 for p in layer), "single-layer model"
    w = _prepare_weights(*[p[0] for p in layer])
    return (_encoder_layer(x_prompt, mem_prompt, w), _encoder_layer(x_sample, mem_sample, w))
```

```python
import functools

import jax
import jax.numpy as jnp
import numpy as np
from jax import lax
from jax.experimental import pallas as pl
from jax.experimental.pallas import tpu as pltpu

F32 = jnp.float32
BF16 = jnp.bfloat16

D_MODEL = 1024
GRID_W = 64
NA_HEADS = 16
NA_HEAD_DIM = 64
WIN_R = 8
WIN_C = 16
D_INNER = 2048
SSD_HEADS = 32
SSD_HEAD_DIM = 64
SSD_GROUPS = 8
SSD_STATE = 128
SSD_CONV = 5
CHUNK = 128
MEM_TOKENS = 256
MEM_HEADS = 4
MEM_HEAD_DIM = 256
D_FF = 2816
RMS_EPS = 1e-6
LOG2E = 1.4426950408889634
NEG_INF = -1e30

LANES = 128
HEADS_PER_GROUP = SSD_HEADS // SSD_GROUPS
GROUP_W = HEADS_PER_GROUP * SSD_HEAD_DIM
PROJ_TILE = 1024
T_Q, T_K, T_V, T_QM, T_Z, T_XS, T_BM, T_CM, T_GATE = 0, 1, 2, 3, 4, 6, 8, 9, 10
N_PROJ_TILES = 13
VMEM_LIMIT = 56 * 1024 * 1024

NT_DIMS = (((1,), (1,)), ((), ()))


def _params(sem, vmem=VMEM_LIMIT):
    return pltpu.CompilerParams(dimension_semantics=sem, vmem_limit_bytes=vmem)


def _const_spec(shape):
    nd = len(shape)
    return pl.BlockSpec(shape, lambda *_: (0,) * nd, pipeline_mode=pl.Buffered(1))


def _sigmoid(x):
    return 1.0 / (1.0 + jnp.exp(-x))


def _silu(x):
    return x * _sigmoid(x)


def _rms(x, g):
    return x * lax.rsqrt(jnp.mean(x * x, axis=-1, keepdims=True) + RMS_EPS) * g


def _ffn_kernel(x_ref, g_ref, wg_ref, wu_ref, wd_ref, g2_ref, o_ref, *u_ref):
    x = x_ref[...]
    xn = _rms(x, g_ref[...]).astype(BF16)
    gate = jnp.dot(xn, wg_ref[...], preferred_element_type=F32)
    up = jnp.dot(xn, wu_ref[...], preferred_element_type=F32)
    h = (_silu(gate) * up).astype(BF16)
    y = x + 0.5 * jnp.dot(h, wd_ref[...], preferred_element_type=F32)
    o_ref[...] = y
    if u_ref:
        u_ref[0][...] = _rms(y, g2_ref[...]).astype(BF16)


def _ffn(x, g, wg, wu, wd, g2, with_u, tm=512):
    n = x.shape[0]
    tok = pl.BlockSpec((tm, D_MODEL), lambda i: (i, 0))
    out_shape = [jax.ShapeDtypeStruct((n, D_MODEL), F32)]
    out_specs = [tok]
    if with_u:
        out_shape.append(jax.ShapeDtypeStruct((n, D_MODEL), BF16))
        out_specs.append(tok)
    res = pl.pallas_call(
        _ffn_kernel,
        grid=(n // tm,),
        in_specs=[tok, _const_spec((1, D_MODEL)), _const_spec((D_MODEL, D_FF)),
                  _const_spec((D_MODEL, D_FF)), _const_spec((D_FF, D_MODEL)), _const_spec((1, D_MODEL))],
        out_specs=out_specs,
        out_shape=out_shape,
        compiler_params=_params(("parallel",)),
        name="ffn_u" if with_u else "ffn",
    )(x, g, wg, wu, wd, g2)
    return res


def _inproj_kernel(u_ref, w_ref, gain_ref, nmat_ref, wdc_ref, wdr_ref, o_ref, dtc_ref, dtr_ref):
    j = pl.program_id(1)
    u = u_ref[...]
    acc = jnp.dot(u, w_ref[...], preferred_element_type=F32)

    @pl.when(j == 0)
    def _():
        dtc_ref[...] = jnp.dot(u, wdc_ref[...], preferred_element_type=F32)
        dtr_ref[...] = lax.dot_general(wdr_ref[...], u, NT_DIMS, preferred_element_type=F32)

    is_norm = (j == T_Q) | (j == T_K) | (j == T_QM)

    @pl.when(is_norm)
    def _():
        nm = nmat_ref[0]
        for c in range(PROJ_TILE // 256):
            a = acc[:, c * 256:(c + 1) * 256]
            ms = jnp.dot((a * a).astype(BF16), nm, preferred_element_type=F32)
            g = gain_ref[0, :, c * 256:(c + 1) * 256]
            o_ref[:, c * 256:(c + 1) * 256] = (a * lax.rsqrt(ms + RMS_EPS) * g).astype(BF16)

    is_silu = (j == T_Z) | (j == T_Z + 1)

    @pl.when(is_silu)
    def _():
        o_ref[...] = _silu(acc).astype(BF16)

    @pl.when(j >= T_GATE)
    def _():
        o_ref[...] = _sigmoid(acc).astype(BF16)

    @pl.when(jnp.logical_not(is_norm | is_silu | (j >= T_GATE)))
    def _():
        o_ref[...] = acc.astype(BF16)


def _inproj(u, w_main, gains, nmats, wdt_c, wdt_r, tm=1024):
    n = u.shape[0]
    return pl.pallas_call(
        _inproj_kernel,
        grid=(n // tm, N_PROJ_TILES),
        in_specs=[
            pl.BlockSpec((tm, D_MODEL), lambda i, j: (i, 0)),
            pl.BlockSpec((D_MODEL, PROJ_TILE), lambda i, j: (0, j)),
            pl.BlockSpec((1, 1, PROJ_TILE), lambda i, j: (jnp.minimum(j, T_QM), 0, 0)),
            pl.BlockSpec((1, 256, 256), lambda i, j: (jnp.where(j == T_QM, 1, 0), 0, 0)),
            pl.BlockSpec((D_MODEL, LANES), lambda i, j: (0, 0)),
            pl.BlockSpec((LANES, D_MODEL), lambda i, j: (0, 0)),
        ],
        out_specs=[
            pl.BlockSpec((tm, PROJ_TILE), lambda i, j: (i, j)),
            pl.BlockSpec((tm, LANES), lambda i, j: (i, 0)),
            pl.BlockSpec((LANES, tm), lambda i, j: (0, i)),
        ],
        out_shape=[
            jax.ShapeDtypeStruct((n, N_PROJ_TILES * PROJ_TILE), BF16),
            jax.ShapeDtypeStruct((n, LANES), F32),
            jax.ShapeDtypeStruct((LANES, n), F32),
        ],
        compiler_params=_params(("parallel", "arbitrary")),
        name="in_proj",
    )(u, w_main, gains, nmats, wdt_c, wdt_r)


def _softplus(x):
    return jnp.maximum(x, 0.0) + jnp.log1p(jnp.exp(-jnp.abs(x)))


def _split3(x):
    p1 = x.astype(BF16)
    r1 = x - p1.astype(F32)
    p2 = r1.astype(BF16)
    p3 = (r1 - p2.astype(F32)).astype(BF16)
    return p1, p2, p3


def _dtprep_kernel(dtc_ref, dtr_ref, bc_ref, ac_ref, fc_ref, br_ref, ar_ref, fr_ref,
                   cola_ref, rowa_ref, roww_ref):
    t = dtc_ref.shape[1]
    li = lax.broadcasted_iota(jnp.int32, (CHUNK, CHUNK), 0)
    ui = lax.broadcasted_iota(jnp.int32, (CHUNK, CHUNK), 1)
    tri_le = (ui <= li).astype(F32).astype(BF16)
    tri_gt = (ui > li).astype(F32).astype(BF16)
    tri_ge = (ui >= li).astype(F32).astype(BF16)
    isf_c = fc_ref[...] > 0.5
    isf_r = fr_ref[...] > 0.5
    for c in range(t // CHUNK):
        sl = slice(c * CHUNK, (c + 1) * CHUNK)
        dt = _softplus(dtc_ref[0, sl, :] + bc_ref[...])
        da = dt * ac_ref[...]
        pre = jnp.zeros((CHUNK, LANES), F32)
        suf = jnp.zeros((CHUNK, LANES), F32)
        for p in _split3(da):
            pre += jnp.dot(tri_le, p, preferred_element_type=F32)
            suf += jnp.dot(tri_gt, p, preferred_element_type=F32)
        cola_ref[0, sl, :] = jnp.where(isf_c, pre, suf + da) * LOG2E
        dtr = _softplus(dtr_ref[:, sl] + br_ref[...])
        dar = dtr * ar_ref[...]
        pre_r = jnp.zeros((LANES, CHUNK), F32)
        suf_r = jnp.zeros((LANES, CHUNK), F32)
        for p in _split3(dar):
            pre_r += jnp.dot(p, tri_ge, preferred_element_type=F32)
            suf_r += lax.dot_general(p, tri_gt, NT_DIMS, preferred_element_type=F32)
        rowa_ref[0, :, sl] = jnp.where(isf_r, pre_r, suf_r + dar) * LOG2E - jnp.log2(dtr)
        roww_ref[0, :, sl] = jnp.exp(jnp.where(isf_r, suf_r, pre_r - dar)) * dtr


def _dtprep(dtc, dtr, bias_c, a_c, isf_c, b, t):
    colspec = pl.BlockSpec((1, t, LANES), lambda i: (i, 0, 0))
    rowspec = pl.BlockSpec((1, LANES, t), lambda i: (i, 0, 0))
    vc = pl.BlockSpec((1, LANES), lambda i: (0, 0))
    vr = pl.BlockSpec((LANES, 1), lambda i: (0, 0))
    return pl.pallas_call(
        _dtprep_kernel,
        grid=(b,),
        in_specs=[colspec, pl.BlockSpec((LANES, t), lambda i: (0, i)), vc, vc, vc, vr, vr, vr],
        out_specs=[colspec, rowspec, rowspec],
        out_shape=[jax.ShapeDtypeStruct((b, t, LANES), F32)] + [jax.ShapeDtypeStruct((b, LANES, t), F32)] * 2,
        compiler_params=_params(("parallel",)),
        name="dt_prep",
    )(dtc.reshape(b, t, LANES), dtr, bias_c, a_c, isf_c, bias_c.reshape(LANES, 1), a_c.reshape(LANES, 1),
      isf_c.reshape(LANES, 1))


NA_ROWS_PER_STEP = 8
NA_KEYS = WIN_R * GRID_W


def _na_kernel(q_ref, k_ref, v_ref, bias_ref, o_ref):
    rows = k_ref.shape[1] // GRID_W
    rb = pl.program_id(1)
    lane = lax.broadcasted_iota(jnp.int32, (GRID_W, LANES), 1)
    lo = lane < NA_HEAD_DIM
    lane1 = lax.broadcasted_iota(jnp.int32, (1, LANES), 1)
    mask_a = (lane1 < NA_HEAD_DIM).astype(F32).astype(BF16)
    mask_b = (lane1 >= NA_HEAD_DIM).astype(F32).astype(BF16)

    def row_body(rl, carry):
        r = rb * NA_ROWS_PER_STEP + rl
        rs = jnp.clip(r - WIN_R // 2, 0, rows - WIN_R)
        s0 = (WIN_R - 1) - (r - rs)
        tbl = (s0 % 2) * 8 + s0 // 2
        q_off = pl.multiple_of(rl * GRID_W, GRID_W)
        k_off = pl.multiple_of(rs * GRID_W, GRID_W)
        def scores(p):
            cs = slice(p * LANES, (p + 1) * LANES)
            q2 = q_ref[0, pl.ds(q_off, GRID_W), cs]
            qblk = jnp.concatenate([q2 * mask_a, q2 * mask_b], axis=0)
            k2 = k_ref[0, pl.ds(k_off, NA_KEYS), cs]
            s = lax.dot_general(qblk, k2, NT_DIMS, preferred_element_type=F32)
            bias = jnp.concatenate([bias_ref[p, tbl + i] for i in range(NA_KEYS // LANES)], axis=1)
            return s + bias

        def attend(p, s):
            cs = slice(p * LANES, (p + 1) * LANES)
            v2 = v_ref[0, pl.ds(k_off, NA_KEYS), cs]
            m = jnp.max(s, axis=1, keepdims=True)
            e = jnp.exp2(s - m)
            l = jnp.sum(e, axis=1, keepdims=True)
            o2 = jnp.dot(e.astype(BF16), v2, preferred_element_type=F32) * (1.0 / l)
            o = jnp.where(lo, o2[:GRID_W], o2[GRID_W:])
            o_ref[0, pl.ds(q_off, GRID_W), cs] = o.astype(BF16)

        n_pairs = NA_HEADS // 2
        s_cur = scores(0)
        for p in range(n_pairs):
            s_next = scores(p + 1) if p + 1 < n_pairs else None
            attend(p, s_cur)
            s_cur = s_next
        return carry

    lax.fori_loop(0, NA_ROWS_PER_STEP, row_body, 0)


def _na(proj, bias_tbl, b, t):
    blk = NA_ROWS_PER_STEP * GRID_W
    return pl.pallas_call(
        _na_kernel,
        grid=(b, t // blk),
        in_specs=[
            pl.BlockSpec((1, blk, PROJ_TILE), lambda i, r: (i, r, T_Q)),
            pl.BlockSpec((1, t, PROJ_TILE), lambda i, r: (i, 0, T_K)),
            pl.BlockSpec((1, t, PROJ_TILE), lambda i, r: (i, 0, T_V)),
            _const_spec(bias_tbl.shape),
        ],
        out_specs=pl.BlockSpec((1, blk, PROJ_TILE), lambda i, r: (i, r, 0)),
        out_shape=jax.ShapeDtypeStruct((b, t, NA_HEADS * NA_HEAD_DIM), BF16),
        compiler_params=_params(("parallel", "arbitrary")),
        name="na_attn",
    )(proj, proj, proj, bias_tbl)


def _na_bias_table(rpb):
    qc = np.arange(GRID_W)[:, None]
    kc = np.arange(GRID_W)[None, :]
    cs = np.clip(qc - WIN_C // 2, 0, GRID_W - WIN_C)
    valid = (kc >= cs) & (kc < cs + WIN_C)
    rel_col = np.clip(kc - qc + WIN_C - 1, 0, 2 * WIN_C - 2)
    t1 = rpb.astype(F32)[:, :, rel_col]
    t1 = jnp.where(jnp.asarray(valid)[None, None], t1 * LOG2E, NEG_INF)
    t1 = jnp.concatenate([t1, jnp.full((NA_HEADS, 3, GRID_W, GRID_W), NEG_INF, F32)], axis=1)
    par = np.arange(2)[:, None, None]
    m = np.arange(8)[None, :, None]
    jj = np.arange(2)[None, None, :]
    rr = 2 * m + par + jj
    t2 = t1[:, rr]
    t2 = t2.reshape(NA_HEADS // 2, 2, 2, 8, 2, GRID_W, GRID_W)
    t2 = jnp.transpose(t2, (0, 2, 3, 1, 5, 4, 6))
    return t2.reshape(NA_HEADS // 2, 16, 2 * GRID_W, 2 * GRID_W)


ROW_A, ROW_W = 0, 8
ROW_PACK = 16
COL_PACK = 2 * HEADS_PER_GROUP


def _ssd_kernel(xs_ref, bm_ref, cm_ref, z_ref, cwx_ref, cbx_ref, cwb_ref, cbb_ref, cwc_ref, cbc_ref,
                col_ref, row_ref, aend_ref, d_ref, nrm_ref, o_ref,
                pad_s, xm_s, bm_s, cm_s, bmt_s, y_s):
    t = xs_ref.shape[1]
    nc = t // CHUNK
    n = SSD_STATE
    half = SSD_CONV // 2
    n_pairs = HEADS_PER_GROUP // 2

    def conv_silu(src_ref, w_ref, b_ref):
        width = src_ref.shape[2]
        pad_s[0:8, :width] = jnp.zeros((8, width), F32)
        pad_s[8 + t:16 + t, :width] = jnp.zeros((8, width), F32)
        pad_s[8:8 + t, :width] = src_ref[0].astype(F32)
        acc = jnp.zeros((t, width), F32) + b_ref[...]
        for k in range(SSD_CONV):
            acc = acc + w_ref[k:k + 1, :] * pad_s[8 - half + k:8 - half + k + t, :width]
        return _silu(acc)

    lo3 = lax.broadcasted_iota(jnp.int32, (nc, CHUNK, LANES), 2) < SSD_HEAD_DIM
    xs = conv_silu(xs_ref, cwx_ref, cbx_ref)
    y_s[...] = (xs * d_ref[...]).reshape(nc, CHUNK, GROUP_W)
    xs3 = xs.reshape(nc, CHUNK, GROUP_W)
    for p in range(n_pairs):
        xp = xs3[:, :, p * LANES:(p + 1) * LANES]
        xm_s[:, p, 0:CHUNK, :] = jnp.where(lo3, xp, 0.0).astype(BF16)
        xm_s[:, p, CHUNK:2 * CHUNK, :] = jnp.where(lo3, 0.0, xp).astype(BF16)
    bm_s[...] = conv_silu(bm_ref, cwb_ref, cbb_ref).astype(BF16).reshape(nc, CHUNK, n)
    cm_s[...] = conv_silu(cm_ref, cwc_ref, cbc_ref).astype(BF16).reshape(nc, CHUNK, n)

    ri = lax.broadcasted_iota(jnp.int32, (n, n), 0)
    ci = lax.broadcasted_iota(jnp.int32, (n, n), 1)
    eye = (ri == ci).astype(F32).astype(BF16)
    for c in range(nc):
        bmt_s[c] = lax.dot_general(eye, bm_s[c], NT_DIMS, preferred_element_type=F32).astype(BF16)

    lower = ci <= ri
    upper = ci >= ri
    lane1 = lax.broadcasted_iota(jnp.int32, (1, LANES), 1)
    mask_a = (lane1 < SSD_HEAD_DIM).astype(F32).astype(BF16)
    mask_b = (lane1 >= SSD_HEAD_DIM).astype(F32).astype(BF16)

    def chunk_gram(c):
        return lax.dot_general(cm_s[c], bm_s[c], NT_DIMS, preferred_element_type=F32)

    def chunk_step(c, direction, h, g):
        mask = lower if direction == 0 else upper
        off = direction * HEADS_PER_GROUP
        ccf = cm_s[c].astype(F32)
        bt = bmt_s[c]
        hb = h.astype(BF16)
        cols = col_ref[0, 0, c]
        rows = row_ref[0, 0, c]
        dstate = []
        for p in range(n_pairs):
            m_parts, c_parts, b_parts = [], [], []
            for hh in (2 * p, 2 * p + 1):
                k = off + hh
                a_col = jnp.broadcast_to(cols[:, k:k + 1], (CHUNK, CHUNK))
                a_row = rows[ROW_A + k:ROW_A + k + 1, :]
                w_row = rows[ROW_W + k:ROW_W + k + 1, :]
                decay = jnp.exp2(jnp.where(mask, a_col - a_row, -jnp.inf))
                m_parts.append((g * decay).astype(BF16))
                c_parts.append((ccf * jnp.exp2(a_col)).astype(BF16))
                b_parts.append(bt * w_row.astype(BF16))
            xm = xm_s[c, p]
            hp = hb[:, p * LANES:(p + 1) * LANES]
            lhs = jnp.concatenate(m_parts + c_parts, axis=1)
            rhs = jnp.concatenate([xm, hp * mask_a, hp * mask_b], axis=0)
            y_s[c, :, p * LANES:(p + 1) * LANES] += jnp.dot(lhs, rhs, preferred_element_type=F32)
            dstate.append(jnp.dot(jnp.concatenate(b_parts, axis=1), xm, preferred_element_type=F32))
        decay_h = jnp.exp2(aend_ref[0, 0, c, direction:direction + 1, :])
        return h * decay_h + jnp.concatenate(dstate, axis=1)

    def body(it, carry):
        h_f, h_b = carry
        g_f = chunk_gram(it)
        g_b = chunk_gram(nc - 1 - it)
        return chunk_step(it, 0, h_f, g_f), chunk_step(nc - 1 - it, 1, h_b, g_b)

    h0 = jnp.zeros((n, GROUP_W), F32)
    lax.fori_loop(0, nc, body, (h0, h0), unroll=2)

    y = y_s[...].reshape(t, GROUP_W) * z_ref[0].astype(F32)
    o_ref[0] = _rms(y, nrm_ref[...]).astype(BF16)


def _ssd(proj, conv_w, conv_b, colpack, rowpack, aend, d_rep, ssd_norm, b, t):
    nc = t // CHUNK
    xs_tile0 = T_XS * PROJ_TILE // GROUP_W
    z_tile0 = T_Z * PROJ_TILE // GROUP_W
    bm_tile0 = T_BM * PROJ_TILE // SSD_STATE
    cm_tile0 = T_CM * PROJ_TILE // SSD_STATE
    cb0 = D_INNER // SSD_STATE
    cc0 = cb0 + SSD_GROUPS
    return pl.pallas_call(
        _ssd_kernel,
        grid=(b, SSD_GROUPS),
        in_specs=[
            pl.BlockSpec((1, t, GROUP_W), lambda i, g: (i, 0, xs_tile0 + g)),
            pl.BlockSpec((1, t, SSD_STATE), lambda i, g: (i, 0, bm_tile0 + g)),
            pl.BlockSpec((1, t, SSD_STATE), lambda i, g: (i, 0, cm_tile0 + g)),
            pl.BlockSpec((1, t, GROUP_W), lambda i, g: (i, 0, z_tile0 + g)),
            pl.BlockSpec((SSD_CONV, GROUP_W), lambda i, g: (0, g)),
            pl.BlockSpec((1, GROUP_W), lambda i, g: (0, g)),
            pl.BlockSpec((SSD_CONV, SSD_STATE), lambda i, g: (0, cb0 + g)),
            pl.BlockSpec((1, SSD_STATE), lambda i, g: (0, cb0 + g)),
            pl.BlockSpec((SSD_CONV, SSD_STATE), lambda i, g: (0, cc0 + g)),
            pl.BlockSpec((1, SSD_STATE), lambda i, g: (0, cc0 + g)),
            pl.BlockSpec((1, 1, nc, CHUNK, COL_PACK), lambda i, g: (i, g, 0, 0, 0)),
            pl.BlockSpec((1, 1, nc, ROW_PACK, CHUNK), lambda i, g: (i, g, 0, 0, 0)),
            pl.BlockSpec((1, 1, nc, 2, GROUP_W), lambda i, g: (i, g, 0, 0, 0)),
            pl.BlockSpec((1, GROUP_W), lambda i, g: (0, g)),
            pl.BlockSpec((1, GROUP_W), lambda i, g: (0, g)),
        ],
        out_specs=pl.BlockSpec((1, t, GROUP_W), lambda i, g: (i, 0, g)),
        out_shape=jax.ShapeDtypeStruct((b, t, D_INNER), BF16),
        scratch_shapes=[
            pltpu.VMEM((t + 16, GROUP_W), F32),
            pltpu.VMEM((nc, HEADS_PER_GROUP // 2, 2 * CHUNK, LANES), BF16),
            pltpu.VMEM((nc, CHUNK, SSD_STATE), BF16),
            pltpu.VMEM((nc, CHUNK, SSD_STATE), BF16),
            pltpu.VMEM((nc, SSD_STATE, CHUNK), BF16),
            pltpu.VMEM((nc, CHUNK, GROUP_W), F32),
        ],
        compiler_params=_params(("parallel", "arbitrary")),
        name="ssd",
    )(proj, proj, proj, proj, conv_w, conv_b, conv_w, conv_b, conv_w, conv_b,
      colpack, rowpack, aend, d_rep, ssd_norm)


def _mem_kernel(q_ref, mem_ref, gm_ref, wkv_ref, gk_ref, o_ref, k_s, v_s):
    width = MEM_HEADS * MEM_HEAD_DIM

    @pl.when(pl.program_id(1) == 0)
    def _():
        mn = _rms(mem_ref[0], gm_ref[...]).astype(BF16)
        kv = jnp.dot(mn, wkv_ref[...], preferred_element_type=F32)
        for h in range(MEM_HEADS):
            hs = slice(h * MEM_HEAD_DIM, (h + 1) * MEM_HEAD_DIM)
            k_s[:, hs] = _rms(kv[:, hs], gk_ref[...]).astype(BF16)
        v_s[...] = kv[:, width:].astype(BF16)

    for h in range(MEM_HEADS):
        hs = slice(h * MEM_HEAD_DIM, (h + 1) * MEM_HEAD_DIM)
        s = lax.dot_general(q_ref[0, :, hs], k_s[:, hs], NT_DIMS, preferred_element_type=F32)
        m = jnp.max(s, axis=1, keepdims=True)
        e = jnp.exp2(s - m)
        l = jnp.sum(e, axis=1, keepdims=True)
        o = jnp.dot(e.astype(BF16), v_s[:, hs], preferred_element_type=F32) * (1.0 / l)
        o_ref[0, :, hs] = o.astype(BF16)


def _mem(proj, mem, g_mem, w_kv, g_k, b, t, tq=512):
    width = MEM_HEADS * MEM_HEAD_DIM
    return pl.pallas_call(
        _mem_kernel,
        grid=(b, t // tq),
        in_specs=[
            pl.BlockSpec((1, tq, PROJ_TILE), lambda i, j: (i, j, T_QM)),
            pl.BlockSpec((1, MEM_TOKENS, D_MODEL), lambda i, j: (i, 0, 0)),
            _const_spec((1, D_MODEL)),
            _const_spec((D_MODEL, 2 * width)),
            _const_spec((1, MEM_HEAD_DIM)),
        ],
        out_specs=pl.BlockSpec((1, tq, width), lambda i, j: (i, j, 0)),
        out_shape=jax.ShapeDtypeStruct((b, t, width), BF16),
        scratch_shapes=[pltpu.VMEM((MEM_TOKENS, width), BF16), pltpu.VMEM((MEM_TOKENS, width), BF16)],
        compiler_params=_params(("parallel", "arbitrary")),
        name="mem_attn",
    )(proj, mem, g_mem, w_kv, g_k)


def _merge_kernel(x_ref, ona_ref, ossd_ref, omem_ref, gna_ref, gssd_ref, gmem_ref,
                  wna_ref, wssd_ref, wmem_ref, wout_ref, o_ref):
    merged = gna_ref[...].astype(F32) * jnp.dot(ona_ref[...], wna_ref[...], preferred_element_type=F32)
    merged += gssd_ref[...].astype(F32) * jnp.dot(ossd_ref[...], wssd_ref[...], preferred_element_type=F32)
    merged += gmem_ref[...].astype(F32) * jnp.dot(omem_ref[...], wmem_ref[...], preferred_element_type=F32)
    o_ref[...] = x_ref[...] + jnp.dot(merged.astype(BF16), wout_ref[...], preferred_element_type=F32)


def _merge(x1, o_na, o_ssd, o_mem, proj, w_na, w_ssd, w_mem, w_out, tm=512):
    n = x1.shape[0]
    tok = lambda w: pl.BlockSpec((tm, w), lambda i: (i, 0))
    gate = lambda k: pl.BlockSpec((tm, PROJ_TILE), lambda i: (i, T_GATE + k))
    return pl.pallas_call(
        _merge_kernel,
        grid=(n // tm,),
        in_specs=[tok(D_MODEL), tok(D_MODEL), tok(D_INNER), tok(D_MODEL), gate(0), gate(1), gate(2),
                  _const_spec((D_MODEL, D_MODEL)), _const_spec((D_INNER, D_MODEL)),
                  _const_spec((D_MODEL, D_MODEL)), _const_spec((D_MODEL, D_MODEL))],
        out_specs=tok(D_MODEL),
        out_shape=jax.ShapeDtypeStruct((n, D_MODEL), F32),
        compiler_params=_params(("parallel",)),
        name="merge_out",
    )(x1, o_na, o_ssd, o_mem, proj, proj, proj, w_na, w_ssd, w_mem, w_out)


def _prepare_weights(ffn1_norm, ffn1_w_gate, ffn1_w_up, ffn1_w_down, mix_norm, w_in, na_q_norm, na_k_norm,
                     na_rpb, conv_w, conv_b, dt_bias_f, dt_bias_b, a_log_f, a_log_b, ssd_d, ssd_norm,
                     mem_norm, w_mem_kv, mem_q_norm, mem_k_norm, w_br_na, w_br_ssd, w_br_mem, w_out,
                     ffn2_norm, ffn2_w_gate, ffn2_w_up, ffn2_w_down):
    bf = lambda w: w.astype(BF16)
    row = lambda v: v.astype(F32).reshape(1, -1)
    dt0 = 3 * NA_HEADS * NA_HEAD_DIM + MEM_HEADS * MEM_HEAD_DIM + D_INNER + D_INNER + 2 * SSD_GROUPS * SSD_STATE
    n_dt = 2 * SSD_HEADS
    w_main = bf(jnp.concatenate([w_in[:, :dt0], w_in[:, dt0 + n_dt:]], axis=1))
    perm = np.concatenate([np.concatenate([np.arange(g * HEADS_PER_GROUP, (g + 1) * HEADS_PER_GROUP),
                                           SSD_HEADS + np.arange(g * HEADS_PER_GROUP, (g + 1) * HEADS_PER_GROUP)])
                           for g in range(SSD_GROUPS)])
    w_dt = w_in[:, dt0:dt0 + n_dt][:, perm]
    wdt_c = bf(jnp.pad(w_dt, ((0, 0), (0, LANES - n_dt))))
    pad_v = lambda v: jnp.pad(v.astype(F32)[perm], (0, LANES - n_dt)).reshape(1, LANES)
    dt_bias = pad_v(jnp.concatenate([dt_bias_f, dt_bias_b]))
    a_neg = pad_v(jnp.concatenate([-jnp.exp(a_log_f.astype(F32)), -jnp.exp(a_log_b.astype(F32))]))
    is_fwd = pad_v(jnp.concatenate([jnp.ones((SSD_HEADS,), F32), jnp.zeros((SSD_HEADS,), F32)]))
    gains = jnp.stack([
        jnp.tile(na_q_norm.astype(F32), NA_HEADS) * (NA_HEAD_DIM ** -0.5 * LOG2E),
        jnp.tile(na_k_norm.astype(F32), NA_HEADS),
        jnp.ones((PROJ_TILE,), F32),
        jnp.tile(mem_q_norm.astype(F32), MEM_HEADS) * (MEM_HEAD_DIM ** -0.5 * LOG2E),
    ]).reshape(4, 1, PROJ_TILE)
    blk = np.arange(256) // NA_HEAD_DIM
    nmat64 = (blk[:, None] == blk[None, :]).astype(np.float32) / NA_HEAD_DIM
    nmat256 = np.full((256, 256), 1.0 / MEM_HEAD_DIM, np.float32)
    nmats = jnp.asarray(np.stack([nmat64, nmat256]), BF16)
    return dict(
        ffn1=(row(ffn1_norm), bf(ffn1_w_gate), bf(ffn1_w_up), bf(ffn1_w_down), row(mix_norm)),
        ffn2=(row(ffn2_norm), bf(ffn2_w_gate), bf(ffn2_w_up), bf(ffn2_w_down), row(ffn2_norm)),
        inproj=(w_main, gains, nmats, wdt_c, wdt_c.T),
        dt=(dt_bias, a_neg, is_fwd),
        na_bias=_na_bias_table(na_rpb),
        conv=(conv_w.astype(F32), row(conv_b)),
        d_rep=jnp.repeat(ssd_d.astype(F32), SSD_HEAD_DIM).reshape(1, D_INNER),
        ssd_norm=row(ssd_norm),
        mem=(row(mem_norm), bf(w_mem_kv), row(mem_k_norm)),
        merge=(bf(w_br_na), bf(w_br_ssd), bf(w_br_mem), bf(w_out)),
    )


def _pack_decay(cola, rowa, roww, b, t):
    nc = t // CHUNK
    nd = 2 * SSD_HEADS
    per_g = 2 * HEADS_PER_GROUP
    colpack = cola[:, :, :nd].reshape(b, nc, CHUNK, SSD_GROUPS, per_g)
    colpack = jnp.transpose(colpack, (0, 3, 1, 2, 4))
    rows = jnp.stack([r[:, :nd].reshape(b, SSD_GROUPS, per_g, nc, CHUNK) for r in (rowa, roww)], axis=2)
    rowpack = jnp.transpose(rows.reshape(b, SSD_GROUPS, ROW_PACK, nc, CHUNK), (0, 1, 3, 2, 4))
    ca = cola.reshape(b, nc, CHUNK, LANES)
    end_f = ca[:, :, CHUNK - 1, :nd].reshape(b, nc, SSD_GROUPS, per_g)[..., :HEADS_PER_GROUP]
    end_b = ca[:, :, 0, :nd].reshape(b, nc, SSD_GROUPS, per_g)[..., HEADS_PER_GROUP:]
    aend = jnp.stack([jnp.repeat(end_f, SSD_HEAD_DIM, axis=-1), jnp.repeat(end_b, SSD_HEAD_DIM, axis=-1)], axis=3)
    aend = jnp.transpose(aend, (0, 2, 1, 3, 4))
    return colpack, rowpack, aend


def _encoder_layer(x, mem, w):
    b, t, _ = x.shape
    n = b * t
    x1, u = _ffn(x.reshape(n, D_MODEL), *w["ffn1"], with_u=True)
    proj, dtc, dtr = _inproj(u, *w["inproj"])
    cola, rowa, roww = _dtprep(dtc, dtr, *w["dt"], b, t)
    colpack, rowpack, aend = _pack_decay(cola, rowa, roww, b, t)
    proj3 = proj.reshape(b, t, N_PROJ_TILES * PROJ_TILE)
    o_na = _na(proj3, w["na_bias"], b, t)
    o_ssd = _ssd(proj3, *w["conv"], colpack, rowpack, aend, w["d_rep"], w["ssd_norm"], b, t)
    o_mem = _mem(proj3, mem, *w["mem"], b, t)
    x2 = _merge(x1, o_na.reshape(n, -1), o_ssd.reshape(n, -1), o_mem.reshape(n, -1), proj, *w["merge"])
    (y,) = _ffn(x2, *w["ffn2"], with_u=False)
    return y.reshape(b, t, D_MODEL)


def kernel(x_prompt, x_sample, mem_prompt, mem_sample, ffn1_norm, ffn1_w_gate, ffn1_w_up, ffn1_w_down, mix_norm, w_in, na_q_norm, na_k_norm, na_rpb, conv_w, conv_b, dt_bias_f, dt_bias_b, a_log_f, a_log_b, ssd_d, ssd_norm, mem_norm, w_mem_kv, mem_q_norm, mem_k_norm, w_br_na, w_br_ssd, w_br_mem, w_out, ffn2_norm, ffn2_w_gate, ffn2_w_up, ffn2_w_down):
    layer = (ffn1_norm, ffn1_w_gate, ffn1_w_up, ffn1_w_down, mix_norm, w_in, na_q_norm, na_k_norm, na_rpb,
             conv_w, conv_b, dt_bias_f, dt_bias_b, a_log_f, a_log_b, ssd_d, ssd_norm, mem_norm, w_mem_kv,
             mem_q_norm, mem_k_norm, w_br_na, w_br_ssd, w_br_mem, w_out, ffn2_norm, ffn2_w_gate, ffn2_w_up,
             ffn2_w_down)
    assert all(p.shape[0] == 1 for p in layer), "single-layer model"
    w = _prepare_weights(*[p[0] for p in layer])
    return (_encoder_layer(x_prompt, mem_prompt, w), _encoder_layer(x_sample, mem_sample, w))
```

```python
import functools

import jax
import jax.numpy as jnp
import numpy as np
from jax import lax
from jax.experimental import pallas as pl
from jax.experimental.pallas import tpu as pltpu

F32 = jnp.float32
BF16 = jnp.bfloat16

D_MODEL = 1024
GRID_W = 64
NA_HEADS = 16
NA_HEAD_DIM = 64
WIN_R = 8
WIN_C = 16
D_INNER = 2048
SSD_HEADS = 32
SSD_HEAD_DIM = 64
SSD_GROUPS = 8
SSD_STATE = 128
SSD_CONV = 5
CHUNK = 128
MEM_TOKENS = 256
MEM_HEADS = 4
MEM_HEAD_DIM = 256
D_FF = 2816
RMS_EPS = 1e-6
LOG2E = 1.4426950408889634
NEG_INF = -1e30

LANES = 128
HEADS_PER_GROUP = SSD_HEADS // SSD_GROUPS
GROUP_W = HEADS_PER_GROUP * SSD_HEAD_DIM
PROJ_TILE = 1024
T_Q, T_K, T_V, T_QM, T_Z, T_XS, T_BM, T_CM, T_GATE = 0, 1, 2, 3, 4, 6, 8, 9, 10
N_PROJ_TILES = 13
VMEM_LIMIT = 56 * 1024 * 1024

NT_DIMS = (((1,), (1,)), ((), ()))


def _params(sem, vmem=VMEM_LIMIT):
    return pltpu.CompilerParams(dimension_semantics=sem, vmem_limit_bytes=vmem)


def _const_spec(shape):
    nd = len(shape)
    return pl.BlockSpec(shape, lambda *_: (0,) * nd, pipeline_mode=pl.Buffered(1))


def _sigmoid(x):
    return 1.0 / (1.0 + jnp.exp(-x))


def _sigmoid_tanh(x):
    return 0.5 + 0.5 * jnp.tanh(0.5 * x)


def _silu(x):
    return x * _sigmoid(x)


def _rms(x, g):
    return x * lax.rsqrt(jnp.mean(x * x, axis=-1, keepdims=True) + RMS_EPS) * g


def _ffn_kernel(x_ref, g_ref, wg_ref, wu_ref, wd_ref, g2_ref, o_ref, *u_ref):
    x = x_ref[...]
    xn = _rms(x, g_ref[...]).astype(BF16)
    gate = jnp.dot(xn, wg_ref[...], preferred_element_type=F32)
    up = jnp.dot(xn, wu_ref[...], preferred_element_type=F32)
    h = (_silu(gate) * up).astype(BF16)
    y = x + 0.5 * jnp.dot(h, wd_ref[...], preferred_element_type=F32)
    o_ref[...] = y
    if u_ref:
        u_ref[0][...] = _rms(y, g2_ref[...]).astype(BF16)


def _ffn(x, g, wg, wu, wd, g2, with_u, tm=512):
    n = x.shape[0]
    tok = pl.BlockSpec((tm, D_MODEL), lambda i: (i, 0))
    out_shape = [jax.ShapeDtypeStruct((n, D_MODEL), F32)]
    out_specs = [tok]
    if with_u:
        out_shape.append(jax.ShapeDtypeStruct((n, D_MODEL), BF16))
        out_specs.append(tok)
    res = pl.pallas_call(
        _ffn_kernel,
        grid=(n // tm,),
        in_specs=[tok, _const_spec((1, D_MODEL)), _const_spec((D_MODEL, D_FF)),
                  _const_spec((D_MODEL, D_FF)), _const_spec((D_FF, D_MODEL)), _const_spec((1, D_MODEL))],
        out_specs=out_specs,
        out_shape=out_shape,
        compiler_params=_params(("parallel",)),
        name="ffn_u" if with_u else "ffn",
    )(x, g, wg, wu, wd, g2)
    return res


def _inproj_kernel(u_ref, w_ref, gain_ref, nmat_ref, wdr_ref, o_ref, dtr_ref):
    j = pl.program_id(1)
    u = u_ref[...]
    acc = jnp.dot(u, w_ref[...], preferred_element_type=F32)

    is_gate = j >= T_GATE
    is_silu = (j == T_Z) | (j == T_Z + 1)
    is_norm = (j == T_Q) | (j == T_K) | (j == T_QM)

    @pl.when(j == 0)
    def _():
        dtr_ref[...] = lax.dot_general(wdr_ref[...], u, NT_DIMS, preferred_element_type=F32)

    @pl.when(is_silu)
    def _():
        o_ref[...] = (acc * _sigmoid_tanh(acc)).astype(BF16)

    @pl.when(is_gate)
    def _():
        o_ref[...] = _sigmoid_tanh(acc).astype(BF16)

    @pl.when(jnp.logical_not(is_norm | is_silu | is_gate))
    def _():
        o_ref[...] = acc.astype(BF16)

    @pl.when(is_norm)
    def _():
        nm = nmat_ref[0]
        for c in range(PROJ_TILE // 256):
            a = acc[:, c * 256:(c + 1) * 256]
            ms = jnp.dot((a * a).astype(BF16), nm, preferred_element_type=F32)
            g = gain_ref[0, :, c * 256:(c + 1) * 256]
            o_ref[:, c * 256:(c + 1) * 256] = (a * lax.rsqrt(ms + RMS_EPS) * g).astype(BF16)


def _inproj(u, w_main, gains, nmats, wdt_r, tm=1024):
    n = u.shape[0]
    return pl.pallas_call(
        _inproj_kernel,
        grid=(n // tm, N_PROJ_TILES),
        in_specs=[
            pl.BlockSpec((tm, D_MODEL), lambda i, j: (i, 0)),
            pl.BlockSpec((D_MODEL, PROJ_TILE), lambda i, j: (0, j)),
            pl.BlockSpec((1, 1, PROJ_TILE), lambda i, j: (jnp.minimum(j, T_QM), 0, 0)),
            pl.BlockSpec((1, 256, 256), lambda i, j: (jnp.where(j == T_QM, 1, 0), 0, 0)),
            pl.BlockSpec((LANES, D_MODEL), lambda i, j: (0, 0)),
        ],
        out_specs=[
            pl.BlockSpec((tm, PROJ_TILE), lambda i, j: (i, j)),
            pl.BlockSpec((LANES, tm), lambda i, j: (0, i)),
        ],
        out_shape=[
            jax.ShapeDtypeStruct((n, N_PROJ_TILES * PROJ_TILE), BF16),
            jax.ShapeDtypeStruct((LANES, n), F32),
        ],
        compiler_params=_params(("parallel", "arbitrary")),
        name="in_proj",
    )(u, w_main, gains, nmats, wdt_r)


def _softplus(x):
    return jnp.maximum(x, 0.0) + jnp.log1p(jnp.exp(-jnp.abs(x)))


def _split3(x):
    p1 = x.astype(BF16)
    r1 = x - p1.astype(F32)
    p2 = r1.astype(BF16)
    p3 = (r1 - p2.astype(F32)).astype(BF16)
    return p1, p2, p3


def _dtprep_kernel(dtr_ref, br_ref, ar_ref, fr_ref, cola_ref, rowa_ref, roww_ref):
    t = dtr_ref.shape[1]
    li = lax.broadcasted_iota(jnp.int32, (CHUNK, CHUNK), 0)
    ui = lax.broadcasted_iota(jnp.int32, (CHUNK, CHUNK), 1)
    tri_gt = (ui > li).astype(F32).astype(BF16)
    tri_ge = (ui >= li).astype(F32).astype(BF16)
    isf_r = fr_ref[...] > 0.5
    for c in range(t // CHUNK):
        sl = slice(c * CHUNK, (c + 1) * CHUNK)
        dtr = _softplus(dtr_ref[:, sl] + br_ref[...])
        dar = dtr * ar_ref[...]
        pre_r = jnp.zeros((LANES, CHUNK), F32)
        suf_r = jnp.zeros((LANES, CHUNK), F32)
        for p in _split3(dar):
            pre_r += jnp.dot(p, tri_ge, preferred_element_type=F32)
            suf_r += lax.dot_general(p, tri_gt, NT_DIMS, preferred_element_type=F32)
        a2 = jnp.where(isf_r, pre_r, suf_r + dar) * LOG2E
        cola_ref[0, sl, :] = a2.T
        rowa_ref[0, :, sl] = a2 - jnp.log2(dtr)
        roww_ref[0, :, sl] = jnp.exp(jnp.where(isf_r, suf_r, pre_r - dar)) * dtr


def _dtprep(dtr, bias_c, a_c, isf_c, b, t):
    colspec = pl.BlockSpec((1, t, LANES), lambda i: (i, 0, 0))
    rowspec = pl.BlockSpec((1, LANES, t), lambda i: (i, 0, 0))
    vr = pl.BlockSpec((LANES, 1), lambda i: (0, 0))
    return pl.pallas_call(
        _dtprep_kernel,
        grid=(b,),
        in_specs=[pl.BlockSpec((LANES, t), lambda i: (0, i)), vr, vr, vr],
        out_specs=[colspec, rowspec, rowspec],
        out_shape=[jax.ShapeDtypeStruct((b, t, LANES), F32)] + [jax.ShapeDtypeStruct((b, LANES, t), F32)] * 2,
        compiler_params=_params(("parallel",)),
        name="dt_prep",
    )(dtr, bias_c.reshape(LANES, 1), a_c.reshape(LANES, 1), isf_c.reshape(LANES, 1))


NA_ROWS_PER_STEP = 8
NA_KEYS = WIN_R * GRID_W
NA_PAIRS_AHEAD = 3


def _na_kernel(q_ref, k_ref, v_ref, bias_ref, o_ref):
    rows = k_ref.shape[1] // GRID_W
    rb = pl.program_id(1)
    lane = lax.broadcasted_iota(jnp.int32, (GRID_W, LANES), 1)
    lo = lane < NA_HEAD_DIM
    lane1 = lax.broadcasted_iota(jnp.int32, (1, LANES), 1)
    mask_a = (lane1 < NA_HEAD_DIM).astype(F32).astype(BF16)
    mask_b = (lane1 >= NA_HEAD_DIM).astype(F32).astype(BF16)

    def row_body(rl, carry):
        r = rb * NA_ROWS_PER_STEP + rl
        rs = jnp.clip(r - WIN_R // 2, 0, rows - WIN_R)
        s0 = (WIN_R - 1) - (r - rs)
        tbl = (s0 % 2) * 8 + s0 // 2
        q_off = pl.multiple_of(rl * GRID_W, GRID_W)
        k_off = pl.multiple_of(rs * GRID_W, GRID_W)
        def scores(p):
            cs = slice(p * LANES, (p + 1) * LANES)
            q2 = q_ref[0, pl.ds(q_off, GRID_W), cs]
            qblk = jnp.concatenate([q2 * mask_a, q2 * mask_b], axis=0)
            k2 = k_ref[0, pl.ds(k_off, NA_KEYS), cs]
            s = lax.dot_general(qblk, k2, NT_DIMS, preferred_element_type=F32)
            bias = jnp.concatenate([bias_ref[p, tbl + i] for i in range(NA_KEYS // LANES)], axis=1)
            return s + bias

        def attend(p, s):
            cs = slice(p * LANES, (p + 1) * LANES)
            v2 = v_ref[0, pl.ds(k_off, NA_KEYS), cs]
            m = jnp.max(s, axis=1, keepdims=True)
            e = jnp.exp2(s - m)
            l = jnp.sum(e, axis=1, keepdims=True)
            o2 = jnp.dot(e.astype(BF16), v2, preferred_element_type=F32) * (1.0 / l)
            o = jnp.where(lo, o2[:GRID_W], o2[GRID_W:])
            o_ref[0, pl.ds(q_off, GRID_W), cs] = o.astype(BF16)

        n_pairs = NA_HEADS // 2
        pending = [scores(p) for p in range(NA_PAIRS_AHEAD)]
        for p in range(n_pairs):
            if p + NA_PAIRS_AHEAD < n_pairs:
                pending.append(scores(p + NA_PAIRS_AHEAD))
            attend(p, pending.pop(0))
        return carry

    lax.fori_loop(0, NA_ROWS_PER_STEP, row_body, 0, unroll=2)


def _na(proj, bias_tbl, b, t):
    blk = NA_ROWS_PER_STEP * GRID_W
    return pl.pallas_call(
        _na_kernel,
        grid=(b, t // blk),
        in_specs=[
            pl.BlockSpec((1, blk, PROJ_TILE), lambda i, r: (i, r, T_Q)),
            pl.BlockSpec((1, t, PROJ_TILE), lambda i, r: (i, 0, T_K)),
            pl.BlockSpec((1, t, PROJ_TILE), lambda i, r: (i, 0, T_V)),
            _const_spec(bias_tbl.shape),
        ],
        out_specs=pl.BlockSpec((1, blk, PROJ_TILE), lambda i, r: (i, r, 0)),
        out_shape=jax.ShapeDtypeStruct((b, t, NA_HEADS * NA_HEAD_DIM), BF16),
        compiler_params=_params(("parallel", "arbitrary")),
        name="na_attn",
    )(proj, proj, proj, bias_tbl)


def _na_bias_table(rpb):
    qc = np.arange(GRID_W)[:, None]
    kc = np.arange(GRID_W)[None, :]
    cs = np.clip(qc - WIN_C // 2, 0, GRID_W - WIN_C)
    valid = (kc >= cs) & (kc < cs + WIN_C)
    rel_col = np.clip(kc - qc + WIN_C - 1, 0, 2 * WIN_C - 2)
    t1 = rpb.astype(F32)[:, :, rel_col]
    t1 = jnp.where(jnp.asarray(valid)[None, None], t1 * LOG2E, NEG_INF)
    t1 = jnp.concatenate([t1, jnp.full((NA_HEADS, 3, GRID_W, GRID_W), NEG_INF, F32)], axis=1)
    par = np.arange(2)[:, None, None]
    m = np.arange(8)[None, :, None]
    jj = np.arange(2)[None, None, :]
    rr = 2 * m + par + jj
    t2 = t1[:, rr]
    t2 = t2.reshape(NA_HEADS // 2, 2, 2, 8, 2, GRID_W, GRID_W)
    t2 = jnp.transpose(t2, (0, 2, 3, 1, 5, 4, 6))
    return t2.reshape(NA_HEADS // 2, 16, 2 * GRID_W, 2 * GRID_W)


ROW_A, ROW_W = 0, 8
ROW_PACK = 16
COL_PACK = 2 * HEADS_PER_GROUP


def _ssd_kernel(xs_ref, bm_ref, cm_ref, z_ref, cwx_ref, cbx_ref, cwb_ref, cbb_ref, cwc_ref, cbc_ref,
                col_ref, row_ref, aend_ref, d_ref, nrm_ref, o_ref,
                conv_in_s, conv_out_s, xm_s, bm_s, cm_s, bmt_s, g_s, y_s):
    t = xs_ref.shape[1]
    nc = t // CHUNK
    n = SSD_STATE
    half = SSD_CONV // 2
    n_pairs = HEADS_PER_GROUP // 2

    seg = conv_in_s.shape[1] // 8 - 2
    slabs = [(xs_ref, 0, cwx_ref, cbx_ref), (xs_ref, LANES, cwx_ref, cbx_ref),
             (bm_ref, 0, cwb_ref, cbb_ref), (cm_ref, 0, cwc_ref, cbc_ref)]
    taps, biases = [], []
    for s, (src_ref, l0, w_ref, b_ref) in enumerate(slabs):
        conv_in_s[s, 0:8, :] = jnp.zeros((8, LANES), F32)
        conv_in_s[s, 8 + t:, :] = jnp.zeros((conv_in_s.shape[1] - 8 - t, LANES), F32)
        conv_in_s[s, 8:8 + t, :] = src_ref[0, :, l0:l0 + LANES].astype(F32)
        taps.append([jnp.broadcast_to(w_ref[k:k + 1, l0:l0 + LANES], (8, LANES)) for k in range(SSD_CONV)])
        biases.append(jnp.broadcast_to(b_ref[:, l0:l0 + LANES], (8, LANES)))

    def conv_body(j, carry):
        for s in range(len(slabs)):
            acc = biases[s]
            for k in range(SSD_CONV):
                acc = acc + taps[s][k] * conv_in_s[s, pl.ds(8 - half + k + j, 8, stride=seg), :]
            hx = 0.5 * acc
            conv_out_s[s, pl.ds(j, 8, stride=seg), :] = hx + hx * jnp.tanh(hx)
        return carry

    lax.fori_loop(0, seg, conv_body, 0, unroll=10)

    lo3 = lax.broadcasted_iota(jnp.int32, (nc, CHUNK, LANES), 2) < SSD_HEAD_DIM
    for p in range(n_pairs):
        xp = conv_out_s[p, 0:t, :].reshape(nc, CHUNK, LANES)
        y_s[:, :, p * LANES:(p + 1) * LANES] = xp * d_ref[:, p * LANES:(p + 1) * LANES]
        xm_s[:, p, 0:CHUNK, :] = jnp.where(lo3, xp, 0.0).astype(BF16)
        xm_s[:, p, CHUNK:2 * CHUNK, :] = jnp.where(lo3, 0.0, xp).astype(BF16)
    bm_s[...] = conv_out_s[2, 0:t, :].astype(BF16).reshape(nc, CHUNK, n)
    cm_s[...] = conv_out_s[3, 0:t, :].astype(BF16).reshape(nc, CHUNK, n)

    ri = lax.broadcasted_iota(jnp.int32, (n, n), 0)
    ci = lax.broadcasted_iota(jnp.int32, (n, n), 1)
    eye = (ri == ci).astype(F32).astype(BF16)
    for c in range(nc):
        bmt_s[c] = lax.dot_general(eye, bm_s[c], NT_DIMS, preferred_element_type=F32).astype(BF16)

    lower = ci <= ri
    upper = ci >= ri
    lo = lax.broadcasted_iota(jnp.int32, (CHUNK, LANES), 1) < SSD_HEAD_DIM

    for c in range(nc):
        g_s[c] = lax.dot_general(cm_s[c], bm_s[c], NT_DIMS, preferred_element_type=F32)

    def chunk_step(c, direction, h, g):
        mask = lower if direction == 0 else upper
        off = direction * HEADS_PER_GROUP
        bt = bmt_s[c]
        ch = jnp.dot(cm_s[c], h.astype(BF16), preferred_element_type=F32)
        cols = col_ref[0, 0, c]
        rows = row_ref[0, 0, c]
        dstate = []
        for p in range(n_pairs):
            m_parts, e_parts, b_parts = [], [], []
            for hh in (2 * p, 2 * p + 1):
                k = off + hh
                a_col = jnp.broadcast_to(cols[:, k:k + 1], (CHUNK, CHUNK))
                a_row = rows[ROW_A + k:ROW_A + k + 1, :]
                w_row = rows[ROW_W + k:ROW_W + k + 1, :]
                decay = jnp.exp2(jnp.where(mask, a_col - a_row, -jnp.inf))
                m_parts.append((g * decay).astype(BF16))
                e_parts.append(jnp.exp2(a_col))
                b_parts.append(bt * w_row.astype(BF16))
            xm = xm_s[c, p]
            y_in = jnp.dot(jnp.concatenate(m_parts, axis=1), xm, preferred_element_type=F32)
            y_off = ch[:, p * LANES:(p + 1) * LANES] * jnp.where(lo, e_parts[0], e_parts[1])
            y_s[c, :, p * LANES:(p + 1) * LANES] += y_in + y_off
            dstate.append(jnp.dot(jnp.concatenate(b_parts, axis=1), xm, preferred_element_type=F32))
        decay_h = jnp.exp2(aend_ref[0, 0, c, direction:direction + 1, :])
        return h * decay_h + jnp.concatenate(dstate, axis=1)

    def body(it, carry):
        h_f, h_b = carry
        return chunk_step(it, 0, h_f, g_s[it]), chunk_step(nc - 1 - it, 1, h_b, g_s[nc - 1 - it])

    h0 = jnp.zeros((n, GROUP_W), F32)
    lax.fori_loop(0, nc, body, (h0, h0), unroll=2)

    y = y_s[...].reshape(t, GROUP_W) * z_ref[0].astype(F32)
    o_ref[0] = _rms(y, nrm_ref[...]).astype(BF16)


def _ssd(proj, conv_w, conv_b, colpack, rowpack, aend, d_rep, ssd_norm, b, t):
    nc = t // CHUNK
    xs_tile0 = T_XS * PROJ_TILE // GROUP_W
    z_tile0 = T_Z * PROJ_TILE // GROUP_W
    bm_tile0 = T_BM * PROJ_TILE // SSD_STATE
    cm_tile0 = T_CM * PROJ_TILE // SSD_STATE
    cb0 = D_INNER // SSD_STATE
    cc0 = cb0 + SSD_GROUPS
    seg = t // 8 + (4 - t // 8) % 8
    return pl.pallas_call(
        _ssd_kernel,
        grid=(b, SSD_GROUPS),
        in_specs=[
            pl.BlockSpec((1, t, GROUP_W), lambda i, g: (i, 0, xs_tile0 + g)),
            pl.BlockSpec((1, t, SSD_STATE), lambda i, g: (i, 0, bm_tile0 + g)),
            pl.BlockSpec((1, t, SSD_STATE), lambda i, g: (i, 0, cm_tile0 + g)),
            pl.BlockSpec((1, t, GROUP_W), lambda i, g: (i, 0, z_tile0 + g)),
            pl.BlockSpec((SSD_CONV, GROUP_W), lambda i, g: (0, g)),
            pl.BlockSpec((1, GROUP_W), lambda i, g: (0, g)),
            pl.BlockSpec((SSD_CONV, SSD_STATE), lambda i, g: (0, cb0 + g)),
            pl.BlockSpec((1, SSD_STATE), lambda i, g: (0, cb0 + g)),
            pl.BlockSpec((SSD_CONV, SSD_STATE), lambda i, g: (0, cc0 + g)),
            pl.BlockSpec((1, SSD_STATE), lambda i, g: (0, cc0 + g)),
            pl.BlockSpec((1, 1, nc, CHUNK, COL_PACK), lambda i, g: (i, g, 0, 0, 0)),
            pl.BlockSpec((1, 1, nc, ROW_PACK, CHUNK), lambda i, g: (i, g, 0, 0, 0)),
            pl.BlockSpec((1, 1, nc, 2, GROUP_W), lambda i, g: (i, g, 0, 0, 0)),
            pl.BlockSpec((1, GROUP_W), lambda i, g: (0, g)),
            pl.BlockSpec((1, GROUP_W), lambda i, g: (0, g)),
        ],
        out_specs=pl.BlockSpec((1, t, GROUP_W), lambda i, g: (i, 0, g)),
        out_shape=jax.ShapeDtypeStruct((b, t, D_INNER), BF16),
        scratch_shapes=[
            pltpu.VMEM((4, 8 * (seg + 2), LANES), F32),
            pltpu.VMEM((4, 8 * seg, LANES), F32),
            pltpu.VMEM((nc, HEADS_PER_GROUP // 2, 2 * CHUNK, LANES), BF16),
            pltpu.VMEM((nc, CHUNK, SSD_STATE), BF16),
            pltpu.VMEM((nc, CHUNK, SSD_STATE), BF16),
            pltpu.VMEM((nc, SSD_STATE, CHUNK), BF16),
            pltpu.VMEM((nc, CHUNK, CHUNK), F32),
            pltpu.VMEM((nc, CHUNK, GROUP_W), F32),
        ],
        compiler_params=_params(("parallel", "arbitrary")),
        name="ssd",
    )(proj, proj, proj, proj, conv_w, conv_b, conv_w, conv_b, conv_w, conv_b,
      colpack, rowpack, aend, d_rep, ssd_norm)


def _mem_kernel(q_ref, mem_ref, gm_ref, wkv_ref, gk_ref, o_ref, k_s, v_s):
    width = MEM_HEADS * MEM_HEAD_DIM

    @pl.when(pl.program_id(1) == 0)
    def _():
        mn = _rms(mem_ref[0], gm_ref[...]).astype(BF16)
        kv = jnp.dot(mn, wkv_ref[...], preferred_element_type=F32)
        for h in range(MEM_HEADS):
            hs = slice(h * MEM_HEAD_DIM, (h + 1) * MEM_HEAD_DIM)
            k_s[:, hs] = _rms(kv[:, hs], gk_ref[...]).astype(BF16)
        v_s[...] = kv[:, width:].astype(BF16)

    for h in range(MEM_HEADS):
        hs = slice(h * MEM_HEAD_DIM, (h + 1) * MEM_HEAD_DIM)
        s = lax.dot_general(q_ref[0, :, hs], k_s[:, hs], NT_DIMS, preferred_element_type=F32)
        m = jnp.max(s, axis=1, keepdims=True)
        e = jnp.exp2(s - m)
        l = jnp.sum(e, axis=1, keepdims=True)
        o = jnp.dot(e.astype(BF16), v_s[:, hs], preferred_element_type=F32) * (1.0 / l)
        o_ref[0, :, hs] = o.astype(BF16)


def _mem(proj, mem, g_mem, w_kv, g_k, b, t, tq=512):
    width = MEM_HEADS * MEM_HEAD_DIM
    return pl.pallas_call(
        _mem_kernel,
        grid=(b, t // tq),
        in_specs=[
            pl.BlockSpec((1, tq, PROJ_TILE), lambda i, j: (i, j, T_QM)),
            pl.BlockSpec((1, MEM_TOKENS, D_MODEL), lambda i, j: (i, 0, 0)),
            _const_spec((1, D_MODEL)),
            _const_spec((D_MODEL, 2 * width)),
            _const_spec((1, MEM_HEAD_DIM)),
        ],
        out_specs=pl.BlockSpec((1, tq, width), lambda i, j: (i, j, 0)),
        out_shape=jax.ShapeDtypeStruct((b, t, width), BF16),
        scratch_shapes=[pltpu.VMEM((MEM_TOKENS, width), BF16), pltpu.VMEM((MEM_TOKENS, width), BF16)],
        compiler_params=_params(("parallel", "arbitrary")),
        name="mem_attn",
    )(proj, mem, g_mem, w_kv, g_k)


def _merge_kernel(x_ref, ona_ref, ossd_ref, omem_ref, gna_ref, gssd_ref, gmem_ref,
                  wna_ref, wssd_ref, wmem_ref, wout_ref, o_ref):
    merged = gna_ref[...].astype(F32) * jnp.dot(ona_ref[...], wna_ref[...], preferred_element_type=F32)
    merged += gssd_ref[...].astype(F32) * jnp.dot(ossd_ref[...], wssd_ref[...], preferred_element_type=F32)
    merged += gmem_ref[...].astype(F32) * jnp.dot(omem_ref[...], wmem_ref[...], preferred_element_type=F32)
    o_ref[...] = x_ref[...] + jnp.dot(merged.astype(BF16), wout_ref[...], preferred_element_type=F32)


def _merge(x1, o_na, o_ssd, o_mem, proj, w_na, w_ssd, w_mem, w_out, tm=512):
    n = x1.shape[0]
    tok = lambda w: pl.BlockSpec((tm, w), lambda i: (i, 0))
    gate = lambda k: pl.BlockSpec((tm, PROJ_TILE), lambda i: (i, T_GATE + k))
    return pl.pallas_call(
        _merge_kernel,
        grid=(n // tm,),
        in_specs=[tok(D_MODEL), tok(D_MODEL), tok(D_INNER), tok(D_MODEL), gate(0), gate(1), gate(2),
                  _const_spec((D_MODEL, D_MODEL)), _const_spec((D_INNER, D_MODEL)),
                  _const_spec((D_MODEL, D_MODEL)), _const_spec((D_MODEL, D_MODEL))],
        out_specs=tok(D_MODEL),
        out_shape=jax.ShapeDtypeStruct((n, D_MODEL), F32),
        compiler_params=_params(("parallel",)),
        name="merge_out",
    )(x1, o_na, o_ssd, o_mem, proj, proj, proj, w_na, w_ssd, w_mem, w_out)


def _prepare_weights(ffn1_norm, ffn1_w_gate, ffn1_w_up, ffn1_w_down, mix_norm, w_in, na_q_norm, na_k_norm,
                     na_rpb, conv_w, conv_b, dt_bias_f, dt_bias_b, a_log_f, a_log_b, ssd_d, ssd_norm,
                     mem_norm, w_mem_kv, mem_q_norm, mem_k_norm, w_br_na, w_br_ssd, w_br_mem, w_out,
                     ffn2_norm, ffn2_w_gate, ffn2_w_up, ffn2_w_down):
    bf = lambda w: w.astype(BF16)
    row = lambda v: v.astype(F32).reshape(1, -1)
    dt0 = 3 * NA_HEADS * NA_HEAD_DIM + MEM_HEADS * MEM_HEAD_DIM + D_INNER + D_INNER + 2 * SSD_GROUPS * SSD_STATE
    n_dt = 2 * SSD_HEADS
    w_main = bf(jnp.concatenate([w_in[:, :dt0], w_in[:, dt0 + n_dt:]], axis=1))
    perm = np.concatenate([np.concatenate([np.arange(g * HEADS_PER_GROUP, (g + 1) * HEADS_PER_GROUP),
                                           SSD_HEADS + np.arange(g * HEADS_PER_GROUP, (g + 1) * HEADS_PER_GROUP)])
                           for g in range(SSD_GROUPS)])
    w_dt = w_in[:, dt0:dt0 + n_dt][:, perm]
    wdt_r = bf(jnp.pad(w_dt, ((0, 0), (0, LANES - n_dt)))).T
    pad_v = lambda v: jnp.pad(v.astype(F32)[perm], (0, LANES - n_dt)).reshape(1, LANES)
    dt_bias = pad_v(jnp.concatenate([dt_bias_f, dt_bias_b]))
    a_neg = pad_v(jnp.concatenate([-jnp.exp(a_log_f.astype(F32)), -jnp.exp(a_log_b.astype(F32))]))
    is_fwd = pad_v(jnp.concatenate([jnp.ones((SSD_HEADS,), F32), jnp.zeros((SSD_HEADS,), F32)]))
    gains = jnp.stack([
        jnp.tile(na_q_norm.astype(F32), NA_HEADS) * (NA_HEAD_DIM ** -0.5 * LOG2E),
        jnp.tile(na_k_norm.astype(F32), NA_HEADS),
        jnp.ones((PROJ_TILE,), F32),
        jnp.tile(mem_q_norm.astype(F32), MEM_HEADS) * (MEM_HEAD_DIM ** -0.5 * LOG2E),
    ]).reshape(4, 1, PROJ_TILE)
    blk = np.arange(256) // NA_HEAD_DIM
    nmat64 = (blk[:, None] == blk[None, :]).astype(np.float32) / NA_HEAD_DIM
    nmat256 = np.full((256, 256), 1.0 / MEM_HEAD_DIM, np.float32)
    nmats = jnp.asarray(np.stack([nmat64, nmat256]), BF16)
    return dict(
        ffn1=(row(ffn1_norm), bf(ffn1_w_gate), bf(ffn1_w_up), bf(ffn1_w_down), row(mix_norm)),
        ffn2=(row(ffn2_norm), bf(ffn2_w_gate), bf(ffn2_w_up), bf(ffn2_w_down), row(ffn2_norm)),
        inproj=(w_main, gains, nmats, wdt_r),
        dt=(dt_bias, a_neg, is_fwd),
        na_bias=_na_bias_table(na_rpb),
        conv=(conv_w.astype(F32), row(conv_b)),
        d_rep=jnp.repeat(ssd_d.astype(F32), SSD_HEAD_DIM).reshape(1, D_INNER),
        ssd_norm=row(ssd_norm),
        mem=(row(mem_norm), bf(w_mem_kv), row(mem_k_norm)),
        merge=(bf(w_br_na), bf(w_br_ssd), bf(w_br_mem), bf(w_out)),
    )


def _pack_decay(cola, rowa, roww, b, t):
    nc = t // CHUNK
    nd = 2 * SSD_HEADS
    per_g = 2 * HEADS_PER_GROUP
    colpack = cola[:, :, :nd].reshape(b, nc, CHUNK, SSD_GROUPS, per_g)
    colpack = jnp.transpose(colpack, (0, 3, 1, 2, 4))
    rows = jnp.stack([r[:, :nd].reshape(b, SSD_GROUPS, per_g, nc, CHUNK) for r in (rowa, roww)], axis=2)
    rowpack = jnp.transpose(rows.reshape(b, SSD_GROUPS, ROW_PACK, nc, CHUNK), (0, 1, 3, 2, 4))
    ca = cola.reshape(b, nc, CHUNK, LANES)
    end_f = ca[:, :, CHUNK - 1, :nd].reshape(b, nc, SSD_GROUPS, per_g)[..., :HEADS_PER_GROUP]
    end_b = ca[:, :, 0, :nd].reshape(b, nc, SSD_GROUPS, per_g)[..., HEADS_PER_GROUP:]
    aend = jnp.stack([jnp.repeat(end_f, SSD_HEAD_DIM, axis=-1), jnp.repeat(end_b, SSD_HEAD_DIM, axis=-1)], axis=3)
    aend = jnp.transpose(aend, (0, 2, 1, 3, 4))
    return colpack, rowpack, aend


def _encoder_layer(x, mem, w):
    b, t, _ = x.shape
    n = b * t
    x1, u = _ffn(x.reshape(n, D_MODEL), *w["ffn1"], with_u=True)
    proj, dtr = _inproj(u, *w["inproj"])
    cola, rowa, roww = _dtprep(dtr, *w["dt"], b, t)
    colpack, rowpack, aend = _pack_decay(cola, rowa, roww, b, t)
    proj3 = proj.reshape(b, t, N_PROJ_TILES * PROJ_TILE)
    o_na = _na(proj3, w["na_bias"], b, t)
    o_ssd = _ssd(proj3, *w["conv"], colpack, rowpack, aend, w["d_rep"], w["ssd_norm"], b, t)
    o_mem = _mem(proj3, mem, *w["mem"], b, t)
    x2 = _merge(x1, o_na.reshape(n, -1), o_ssd.reshape(n, -1), o_mem.reshape(n, -1), proj, *w["merge"])
    (y,) = _ffn(x2, *w["ffn2"], with_u=False)
    return y.reshape(b, t, D_MODEL)


def kernel(x_prompt, x_sample, mem_prompt, mem_sample, ffn1_norm, ffn1_w_gate, ffn1_w_up, ffn1_w_down, mix_norm, w_in, na_q_norm, na_k_norm, na_rpb, conv_w, conv_b, dt_bias_f, dt_bias_b, a_log_f, a_log_b, ssd_d, ssd_norm, mem_norm, w_mem_kv, mem_q_norm, mem_k_norm, w_br_na, w_br_ssd, w_br_mem, w_out, ffn2_norm, ffn2_w_gate, ffn2_w_up, ffn2_w_down):
    layer = (ffn1_norm, ffn1_w_gate, ffn1_w_up, ffn1_w_down, mix_norm, w_in, na_q_norm, na_k_norm, na_rpb,
             conv_w, conv_b, dt_bias_f, dt_bias_b, a_log_f, a_log_b, ssd_d, ssd_norm, mem_norm, w_mem_kv,
             mem_q_norm, mem_k_norm, w_br_na, w_br_ssd, w_br_mem, w_out, ffn2_norm, ffn2_w_gate, ffn2_w_up,
             ffn2_w_down)
    assert all(p.shape[0] == 1 for p in layer), "single-layer model"
    w = _prepare_weights(*[p[0] for p in layer])
    return (_encoder_layer(x_prompt, mem_prompt, w), _encoder_layer(x_sample, mem_sample, w))
```

```python
import functools

import jax
import jax.numpy as jnp
import numpy as np
from jax import lax
from jax.experimental import pallas as pl
from jax.experimental.pallas import tpu as pltpu

F32 = jnp.float32
BF16 = jnp.bfloat16

D_MODEL = 1024
GRID_W = 64
NA_HEADS = 16
NA_HEAD_DIM = 64
WIN_R = 8
WIN_C = 16
D_INNER = 2048
SSD_HEADS = 32
SSD_HEAD_DIM = 64
SSD_GROUPS = 8
SSD_STATE = 128
SSD_CONV = 5
CHUNK = 128
MEM_TOKENS = 256
MEM_HEADS = 4
MEM_HEAD_DIM = 256
D_FF = 2816
RMS_EPS = 1e-6
LOG2E = 1.4426950408889634
NEG_INF = -1e30

LANES = 128
HEADS_PER_GROUP = SSD_HEADS // SSD_GROUPS
GROUP_W = HEADS_PER_GROUP * SSD_HEAD_DIM
PROJ_TILE = 1024
T_Q, T_K, T_V, T_QM, T_Z, T_XS, T_BM, T_CM, T_GATE = 0, 1, 2, 3, 4, 6, 8, 9, 10
N_PROJ_TILES = 13
VMEM_LIMIT = 56 * 1024 * 1024

NT_DIMS = (((1,), (1,)), ((), ()))


def _params(sem, vmem=VMEM_LIMIT):
    return pltpu.CompilerParams(dimension_semantics=sem, vmem_limit_bytes=vmem)


def _const_spec(shape):
    nd = len(shape)
    return pl.BlockSpec(shape, lambda *_: (0,) * nd, pipeline_mode=pl.Buffered(1))


def _sigmoid(x):
    return 1.0 / (1.0 + jnp.exp(-x))


def _sigmoid_tanh(x):
    return 0.5 + 0.5 * jnp.tanh(0.5 * x)


def _silu(x):
    return x * _sigmoid(x)


def _rms(x, g):
    return x * lax.rsqrt(jnp.mean(x * x, axis=-1, keepdims=True) + RMS_EPS) * g


def _ffn_kernel(x_ref, g_ref, wg_ref, wu_ref, wd_ref, g2_ref, o_ref, *u_ref):
    x = x_ref[...]
    xn = _rms(x, g_ref[...]).astype(BF16)
    gate = jnp.dot(xn, wg_ref[...], preferred_element_type=F32)
    up = jnp.dot(xn, wu_ref[...], preferred_element_type=F32)
    h = (_silu(gate) * up).astype(BF16)
    y = x + 0.5 * jnp.dot(h, wd_ref[...], preferred_element_type=F32)
    o_ref[...] = y
    if u_ref:
        u_ref[0][...] = _rms(y, g2_ref[...]).astype(BF16)


def _ffn(x, g, wg, wu, wd, g2, with_u, tm=512):
    n = x.shape[0]
    tok = pl.BlockSpec((tm, D_MODEL), lambda i: (i, 0))
    out_shape = [jax.ShapeDtypeStruct((n, D_MODEL), F32)]
    out_specs = [tok]
    if with_u:
        out_shape.append(jax.ShapeDtypeStruct((n, D_MODEL), BF16))
        out_specs.append(tok)
    res = pl.pallas_call(
        _ffn_kernel,
        grid=(n // tm,),
        in_specs=[tok, _const_spec((1, D_MODEL)), _const_spec((D_MODEL, D_FF)),
                  _const_spec((D_MODEL, D_FF)), _const_spec((D_FF, D_MODEL)), _const_spec((1, D_MODEL))],
        out_specs=out_specs,
        out_shape=out_shape,
        compiler_params=_params(("parallel",)),
        name="ffn_u" if with_u else "ffn",
    )(x, g, wg, wu, wd, g2)
    return res


NORM_W = 256


def _inproj_kernel(u_ref, w_ref, gain_ref, nmat_ref, wdr_ref, o_ref, dtr_ref):
    j = pl.program_id(1)
    is_gate = j >= T_GATE
    is_silu = (j == T_Z) | (j == T_Z + 1)
    is_norm = (j == T_Q) | (j == T_K) | (j == T_QM)

    @pl.when(j == 0)
    def _():
        dtr_ref[...] = lax.dot_general(wdr_ref[...], u_ref[...], NT_DIMS, preferred_element_type=F32)

    def tile(epilogue):
        u = u_ref[...]
        for c in range(PROJ_TILE // NORM_W):
            cs = slice(c * NORM_W, (c + 1) * NORM_W)
            acc = jnp.dot(u, w_ref[:, cs], preferred_element_type=F32)
            o_ref[:, cs] = epilogue(acc, cs).astype(BF16)

    @pl.when(is_norm)
    def _():
        acc = jnp.dot(u_ref[...], w_ref[...], preferred_element_type=F32)
        for c in range(PROJ_TILE // NORM_W):
            cs = slice(c * NORM_W, (c + 1) * NORM_W)
            a = acc[:, cs]
            ms = jnp.dot((a * a).astype(BF16), nmat_ref[0], preferred_element_type=F32)
            o_ref[:, cs] = (a * lax.rsqrt(ms + RMS_EPS) * gain_ref[0, :, cs]).astype(BF16)

    pl.when(is_silu)(lambda: tile(lambda a, cs: a * _sigmoid_tanh(a)))
    pl.when(is_gate)(lambda: tile(lambda a, cs: _sigmoid_tanh(a)))
    pl.when(jnp.logical_not(is_norm | is_silu | is_gate))(lambda: tile(lambda a, cs: a))


def _inproj(u, w_main, gains, nmats, wdt_r, tm=1024):
    n = u.shape[0]
    return pl.pallas_call(
        _inproj_kernel,
        grid=(n // tm, N_PROJ_TILES),
        in_specs=[
            pl.BlockSpec((tm, D_MODEL), lambda i, j: (i, 0)),
            pl.BlockSpec((D_MODEL, PROJ_TILE), lambda i, j: (0, j)),
            pl.BlockSpec((1, 1, PROJ_TILE), lambda i, j: (jnp.minimum(j, T_QM), 0, 0)),
            pl.BlockSpec((1, 256, 256), lambda i, j: (jnp.where(j == T_QM, 1, 0), 0, 0)),
            pl.BlockSpec((LANES, D_MODEL), lambda i, j: (0, 0)),
        ],
        out_specs=[
            pl.BlockSpec((tm, PROJ_TILE), lambda i, j: (i, j)),
            pl.BlockSpec((LANES, tm), lambda i, j: (0, i)),
        ],
        out_shape=[
            jax.ShapeDtypeStruct((n, N_PROJ_TILES * PROJ_TILE), BF16),
            jax.ShapeDtypeStruct((LANES, n), F32),
        ],
        compiler_params=_params(("parallel", "arbitrary")),
        name="in_proj",
    )(u, w_main, gains, nmats, wdt_r)


def _softplus(x):
    return jnp.maximum(x, 0.0) + jnp.log1p(jnp.exp(-jnp.abs(x)))


def _split3(x):
    p1 = x.astype(BF16)
    r1 = x - p1.astype(F32)
    p2 = r1.astype(BF16)
    p3 = (r1 - p2.astype(F32)).astype(BF16)
    return p1, p2, p3


ROW_A, ROW_W, ROW_A2 = 0, 8, 16
ROW_PACK = 24


def _dtprep_kernel(dtr_ref, br_ref, ar_ref, fr_ref, pack_ref):
    t = dtr_ref.shape[1]
    per_g = 2 * HEADS_PER_GROUP
    li = lax.broadcasted_iota(jnp.int32, (CHUNK, CHUNK), 0)
    ui = lax.broadcasted_iota(jnp.int32, (CHUNK, CHUNK), 1)
    tri_gt = (ui > li).astype(F32).astype(BF16)
    tri_ge = (ui >= li).astype(F32).astype(BF16)
    isf_r = fr_ref[...] > 0.5
    for c in range(t // CHUNK):
        sl = slice(c * CHUNK, (c + 1) * CHUNK)
        dtr = _softplus(dtr_ref[:, sl] + br_ref[...])
        dar = dtr * ar_ref[...]
        pre_r = jnp.zeros((LANES, CHUNK), F32)
        suf_r = jnp.zeros((LANES, CHUNK), F32)
        for p in _split3(dar):
            pre_r += jnp.dot(p, tri_ge, preferred_element_type=F32)
            suf_r += lax.dot_general(p, tri_gt, NT_DIMS, preferred_element_type=F32)
        a2 = jnp.where(isf_r, pre_r, suf_r + dar) * LOG2E
        a_row = a2 - jnp.log2(dtr)
        w_row = jnp.exp(jnp.where(isf_r, suf_r, pre_r - dar)) * dtr
        for g in range(SSD_GROUPS):
            gs = slice(g * per_g, (g + 1) * per_g)
            pack_ref[0, g, c, ROW_A:ROW_A + per_g, :] = a_row[gs]
            pack_ref[0, g, c, ROW_W:ROW_W + per_g, :] = w_row[gs]
            pack_ref[0, g, c, ROW_A2:ROW_A2 + per_g, :] = a2[gs]


def _dtprep(dtr, bias_c, a_c, isf_c, b, t):
    nc = t // CHUNK
    vr = pl.BlockSpec((LANES, 1), lambda i: (0, 0))
    return pl.pallas_call(
        _dtprep_kernel,
        grid=(b,),
        in_specs=[pl.BlockSpec((LANES, t), lambda i: (0, i)), vr, vr, vr],
        out_specs=pl.BlockSpec((1, SSD_GROUPS, nc, ROW_PACK, CHUNK), lambda i: (i, 0, 0, 0, 0)),
        out_shape=jax.ShapeDtypeStruct((b, SSD_GROUPS, nc, ROW_PACK, CHUNK), F32),
        compiler_params=_params(("parallel",)),
        name="dt_prep",
    )(dtr, bias_c.reshape(LANES, 1), a_c.reshape(LANES, 1), isf_c.reshape(LANES, 1))


NA_ROWS_PER_STEP = 8
NA_KEYS = WIN_R * GRID_W
NA_PAIRS_AHEAD = 3


def _na_kernel(q_ref, k_ref, v_ref, bias_ref, o_ref):
    rows = k_ref.shape[1] // GRID_W
    rb = pl.program_id(1)
    lane = lax.broadcasted_iota(jnp.int32, (GRID_W, LANES), 1)
    lo = lane < NA_HEAD_DIM
    lane1 = lax.broadcasted_iota(jnp.int32, (1, LANES), 1)
    mask_a = (lane1 < NA_HEAD_DIM).astype(F32).astype(BF16)
    mask_b = (lane1 >= NA_HEAD_DIM).astype(F32).astype(BF16)

    def row_body(rl, carry):
        r = rb * NA_ROWS_PER_STEP + rl
        rs = jnp.clip(r - WIN_R // 2, 0, rows - WIN_R)
        s0 = (WIN_R - 1) - (r - rs)
        tbl = (s0 % 2) * 8 + s0 // 2
        q_off = pl.multiple_of(rl * GRID_W, GRID_W)
        k_off = pl.multiple_of(rs * GRID_W, GRID_W)
        def scores(p):
            cs = slice(p * LANES, (p + 1) * LANES)
            q2 = q_ref[0, pl.ds(q_off, GRID_W), cs]
            qblk = jnp.concatenate([q2 * mask_a, q2 * mask_b], axis=0)
            k2 = k_ref[0, pl.ds(k_off, NA_KEYS), cs]
            s = lax.dot_general(qblk, k2, NT_DIMS, preferred_element_type=F32)
            bias = jnp.concatenate([bias_ref[p, tbl + i] for i in range(NA_KEYS // LANES)], axis=1)
            return s + bias

        def attend(p, s):
            cs = slice(p * LANES, (p + 1) * LANES)
            v2 = v_ref[0, pl.ds(k_off, NA_KEYS), cs]
            m = jnp.max(s, axis=1, keepdims=True)
            e = jnp.exp2(s - m)
            l = jnp.sum(e, axis=1, keepdims=True)
            o2 = jnp.dot(e.astype(BF16), v2, preferred_element_type=F32) * (1.0 / l)
            o = jnp.where(lo, o2[:GRID_W], o2[GRID_W:])
            o_ref[0, pl.ds(q_off, GRID_W), cs] = o.astype(BF16)

        n_pairs = NA_HEADS // 2
        pending = [scores(p) for p in range(NA_PAIRS_AHEAD)]
        for p in range(n_pairs):
            if p + NA_PAIRS_AHEAD < n_pairs:
                pending.append(scores(p + NA_PAIRS_AHEAD))
            attend(p, pending.pop(0))
        return carry

    lax.fori_loop(0, NA_ROWS_PER_STEP, row_body, 0, unroll=2)


def _na(proj, bias_tbl, b, t):
    blk = NA_ROWS_PER_STEP * GRID_W
    return pl.pallas_call(
        _na_kernel,
        grid=(b, t // blk),
        in_specs=[
            pl.BlockSpec((1, blk, PROJ_TILE), lambda i, r: (i, r, T_Q)),
            pl.BlockSpec((1, t, PROJ_TILE), lambda i, r: (i, 0, T_K)),
            pl.BlockSpec((1, t, PROJ_TILE), lambda i, r: (i, 0, T_V)),
            _const_spec(bias_tbl.shape),
        ],
        out_specs=pl.BlockSpec((1, blk, PROJ_TILE), lambda i, r: (i, r, 0)),
        out_shape=jax.ShapeDtypeStruct((b, t, NA_HEADS * NA_HEAD_DIM), BF16),
        compiler_params=_params(("parallel", "arbitrary")),
        name="na_attn",
    )(proj, proj, proj, bias_tbl)


def _na_bias_table(rpb):
    qc = np.arange(GRID_W)[:, None]
    kc = np.arange(GRID_W)[None, :]
    cs = np.clip(qc - WIN_C // 2, 0, GRID_W - WIN_C)
    valid = (kc >= cs) & (kc < cs + WIN_C)
    rel_col = np.clip(kc - qc + WIN_C - 1, 0, 2 * WIN_C - 2)
    t1 = rpb.astype(F32)[:, :, rel_col]
    t1 = jnp.where(jnp.asarray(valid)[None, None], t1 * LOG2E, NEG_INF)
    t1 = jnp.concatenate([t1, jnp.full((NA_HEADS, 3, GRID_W, GRID_W), NEG_INF, F32)], axis=1)
    par = np.arange(2)[:, None, None]
    m = np.arange(8)[None, :, None]
    jj = np.arange(2)[None, None, :]
    rr = 2 * m + par + jj
    t2 = t1[:, rr]
    t2 = t2.reshape(NA_HEADS // 2, 2, 2, 8, 2, GRID_W, GRID_W)
    t2 = jnp.transpose(t2, (0, 2, 3, 1, 5, 4, 6))
    return t2.reshape(NA_HEADS // 2, 16, 2 * GRID_W, 2 * GRID_W)


def _ssd_kernel(xs_ref, bm_ref, cm_ref, z_ref, cwx_ref, cbx_ref, cwb_ref, cbb_ref, cwc_ref, cbc_ref,
                row_ref, d_ref, nrm_ref, o_ref,
                conv_in_s, conv_out_s, xm_s, bm_s, cm_s, bmt_s, g_s, y_s):
    t = xs_ref.shape[1]
    nc = t // CHUNK
    n = SSD_STATE
    half = SSD_CONV // 2
    n_pairs = HEADS_PER_GROUP // 2

    seg = conv_in_s.shape[1] // 8 - 2
    slabs = [(xs_ref, 0, cwx_ref, cbx_ref), (xs_ref, LANES, cwx_ref, cbx_ref),
             (bm_ref, 0, cwb_ref, cbb_ref), (cm_ref, 0, cwc_ref, cbc_ref)]
    taps, biases = [], []
    for s, (src_ref, l0, w_ref, b_ref) in enumerate(slabs):
        conv_in_s[s, 0:8, :] = jnp.zeros((8, LANES), F32)
        conv_in_s[s, 8 + t:, :] = jnp.zeros((conv_in_s.shape[1] - 8 - t, LANES), F32)
        conv_in_s[s, 8:8 + t, :] = src_ref[0, :, l0:l0 + LANES].astype(F32)
        taps.append([jnp.broadcast_to(0.5 * w_ref[k:k + 1, l0:l0 + LANES], (8, LANES)) for k in range(SSD_CONV)])
        biases.append(jnp.broadcast_to(0.5 * b_ref[:, l0:l0 + LANES], (8, LANES)))

    def conv_body(j, carry):
        for s in range(len(slabs)):
            acc = biases[s]
            for k in range(SSD_CONV):
                acc = acc + taps[s][k] * conv_in_s[s, pl.ds(8 - half + k + j, 8, stride=seg), :]
            conv_out_s[s, pl.ds(j, 8, stride=seg), :] = acc + acc * jnp.tanh(acc)
        return carry

    lax.fori_loop(0, seg, conv_body, 0, unroll=10)

    lo3 = lax.broadcasted_iota(jnp.int32, (nc, CHUNK, LANES), 2) < SSD_HEAD_DIM
    for p in range(n_pairs):
        xp = conv_out_s[p, 0:t, :].reshape(nc, CHUNK, LANES)
        y_s[:, :, p * LANES:(p + 1) * LANES] = xp * d_ref[:, p * LANES:(p + 1) * LANES]
        xm_s[:, p, 0:CHUNK, :] = jnp.where(lo3, xp, 0.0).astype(BF16)
        xm_s[:, p, CHUNK:2 * CHUNK, :] = jnp.where(lo3, 0.0, xp).astype(BF16)
    bm_s[...] = conv_out_s[2, 0:t, :].astype(BF16).reshape(nc, CHUNK, n)
    cm_s[...] = conv_out_s[3, 0:t, :].astype(BF16).reshape(nc, CHUNK, n)

    ri = lax.broadcasted_iota(jnp.int32, (n, n), 0)
    ci = lax.broadcasted_iota(jnp.int32, (n, n), 1)
    eye = (ri == ci).astype(F32).astype(BF16)
    for c in range(nc):
        bmt_s[c] = lax.dot_general(eye, bm_s[c], NT_DIMS, preferred_element_type=F32).astype(BF16)

    lower = ci <= ri
    upper = ci >= ri
    lo = lax.broadcasted_iota(jnp.int32, (CHUNK, LANES), 1) < SSD_HEAD_DIM
    lo1 = lax.broadcasted_iota(jnp.int32, (1, LANES), 1) < SSD_HEAD_DIM

    for c in range(nc):
        g_s[c] = lax.dot_general(cm_s[c], bm_s[c], NT_DIMS, preferred_element_type=F32)

    def chunk_step(c, direction, h, g):
        mask = lower if direction == 0 else upper
        off = direction * HEADS_PER_GROUP
        bt = bmt_s[c]
        ch = jnp.dot(cm_s[c], h.astype(BF16), preferred_element_type=F32)
        rows = row_ref[0, 0, c]
        edge = CHUNK - 1 if direction == 0 else 0
        dstate, decay_h = [], []
        for p in range(n_pairs):
            m_parts, e_parts, b_parts = [], [], []
            for hh in (2 * p, 2 * p + 1):
                k = off + hh
                a_col = jnp.broadcast_to(rows[ROW_A2 + k:ROW_A2 + k + 1, :], (CHUNK, CHUNK)).T
                a_row = rows[ROW_A + k:ROW_A + k + 1, :]
                w_row = rows[ROW_W + k:ROW_W + k + 1, :]
                decay = jnp.exp2(jnp.where(mask, a_col - a_row, -jnp.inf))
                m_parts.append((g * decay).astype(BF16))
                e_parts.append(jnp.exp2(a_col))
                b_parts.append(bt * w_row.astype(BF16))
            xm = xm_s[c, p]
            y_in = jnp.dot(jnp.concatenate(m_parts, axis=1), xm, preferred_element_type=F32)
            y_off = ch[:, p * LANES:(p + 1) * LANES] * jnp.where(lo, e_parts[0], e_parts[1])
            y_s[c, :, p * LANES:(p + 1) * LANES] += y_in + y_off
            dstate.append(jnp.dot(jnp.concatenate(b_parts, axis=1), xm, preferred_element_type=F32))
            decay_h.append(jnp.where(lo1, e_parts[0][edge:edge + 1, :], e_parts[1][edge:edge + 1, :]))
        return h * jnp.concatenate(decay_h, axis=1) + jnp.concatenate(dstate, axis=1)

    def body(it, carry):
        h_f, h_b = carry
        return chunk_step(it, 0, h_f, g_s[it]), chunk_step(nc - 1 - it, 1, h_b, g_s[nc - 1 - it])

    h0 = jnp.zeros((n, GROUP_W), F32)
    lax.fori_loop(0, nc, body, (h0, h0), unroll=8)

    y = y_s[...].reshape(t, GROUP_W) * z_ref[0].astype(F32)
    o_ref[0] = _rms(y, nrm_ref[...]).astype(BF16)


def _ssd(proj, conv_w, conv_b, rowpack, d_rep, ssd_norm, b, t):
    nc = t // CHUNK
    xs_tile0 = T_XS * PROJ_TILE // GROUP_W
    z_tile0 = T_Z * PROJ_TILE // GROUP_W
    bm_tile0 = T_BM * PROJ_TILE // SSD_STATE
    cm_tile0 = T_CM * PROJ_TILE // SSD_STATE
    cb0 = D_INNER // SSD_STATE
    cc0 = cb0 + SSD_GROUPS
    seg = t // 8 + (4 - t // 8) % 8
    return pl.pallas_call(
        _ssd_kernel,
        grid=(b, SSD_GROUPS),
        in_specs=[
            pl.BlockSpec((1, t, GROUP_W), lambda i, g: (i, 0, xs_tile0 + g)),
            pl.BlockSpec((1, t, SSD_STATE), lambda i, g: (i, 0, bm_tile0 + g)),
            pl.BlockSpec((1, t, SSD_STATE), lambda i, g: (i, 0, cm_tile0 + g)),
            pl.BlockSpec((1, t, GROUP_W), lambda i, g: (i, 0, z_tile0 + g)),
            pl.BlockSpec((SSD_CONV, GROUP_W), lambda i, g: (0, g)),
            pl.BlockSpec((1, GROUP_W), lambda i, g: (0, g)),
            pl.BlockSpec((SSD_CONV, SSD_STATE), lambda i, g: (0, cb0 + g)),
            pl.BlockSpec((1, SSD_STATE), lambda i, g: (0, cb0 + g)),
            pl.BlockSpec((SSD_CONV, SSD_STATE), lambda i, g: (0, cc0 + g)),
            pl.BlockSpec((1, SSD_STATE), lambda i, g: (0, cc0 + g)),
            pl.BlockSpec((1, 1, nc, ROW_PACK, CHUNK), lambda i, g: (i, g, 0, 0, 0)),
            pl.BlockSpec((1, GROUP_W), lambda i, g: (0, g)),
            pl.BlockSpec((1, GROUP_W), lambda i, g: (0, g)),
        ],
        out_specs=pl.BlockSpec((1, t, GROUP_W), lambda i, g: (i, 0, g)),
        out_shape=jax.ShapeDtypeStruct((b, t, D_INNER), BF16),
        scratch_shapes=[
            pltpu.VMEM((4, 8 * (seg + 2), LANES), F32),
            pltpu.VMEM((4, 8 * seg, LANES), F32),
            pltpu.VMEM((nc, HEADS_PER_GROUP // 2, 2 * CHUNK, LANES), BF16),
            pltpu.VMEM((nc, CHUNK, SSD_STATE), BF16),
            pltpu.VMEM((nc, CHUNK, SSD_STATE), BF16),
            pltpu.VMEM((nc, SSD_STATE, CHUNK), BF16),
            pltpu.VMEM((nc, CHUNK, CHUNK), F32),
            pltpu.VMEM((nc, CHUNK, GROUP_W), F32),
        ],
        compiler_params=_params(("parallel", "arbitrary")),
        name="ssd",
    )(proj, proj, proj, proj, conv_w, conv_b, conv_w, conv_b, conv_w, conv_b,
      rowpack, d_rep, ssd_norm)


def _mem_kernel(q_ref, mem_ref, gm_ref, wkv_ref, gk_ref, o_ref, k_s, v_s):
    width = MEM_HEADS * MEM_HEAD_DIM

    @pl.when(pl.program_id(1) == 0)
    def _():
        mn = _rms(mem_ref[0], gm_ref[...]).astype(BF16)
        kv = jnp.dot(mn, wkv_ref[...], preferred_element_type=F32)
        for h in range(MEM_HEADS):
            hs = slice(h * MEM_HEAD_DIM, (h + 1) * MEM_HEAD_DIM)
            k_s[:, hs] = _rms(kv[:, hs], gk_ref[...]).astype(BF16)
        v_s[...] = kv[:, width:].astype(BF16)

    for h in range(MEM_HEADS):
        hs = slice(h * MEM_HEAD_DIM, (h + 1) * MEM_HEAD_DIM)
        s = lax.dot_general(q_ref[0, :, hs], k_s[:, hs], NT_DIMS, preferred_element_type=F32)
        m = jnp.max(s, axis=1, keepdims=True)
        e = jnp.exp2(s - m)
        l = jnp.sum(e, axis=1, keepdims=True)
        o = jnp.dot(e.astype(BF16), v_s[:, hs], preferred_element_type=F32) * (1.0 / l)
        o_ref[0, :, hs] = o.astype(BF16)


def _mem(proj, mem, g_mem, w_kv, g_k, b, t, tq=512):
    width = MEM_HEADS * MEM_HEAD_DIM
    return pl.pallas_call(
        _mem_kernel,
        grid=(b, t // tq),
        in_specs=[
            pl.BlockSpec((1, tq, PROJ_TILE), lambda i, j: (i, j, T_QM)),
            pl.BlockSpec((1, MEM_TOKENS, D_MODEL), lambda i, j: (i, 0, 0)),
            _const_spec((1, D_MODEL)),
            _const_spec((D_MODEL, 2 * width)),
            _const_spec((1, MEM_HEAD_DIM)),
        ],
        out_specs=pl.BlockSpec((1, tq, width), lambda i, j: (i, j, 0)),
        out_shape=jax.ShapeDtypeStruct((b, t, width), BF16),
        scratch_shapes=[pltpu.VMEM((MEM_TOKENS, width), BF16), pltpu.VMEM((MEM_TOKENS, width), BF16)],
        compiler_params=_params(("parallel", "arbitrary")),
        name="mem_attn",
    )(proj, mem, g_mem, w_kv, g_k)


def _merge_kernel(x_ref, ona_ref, ossd_ref, omem_ref, gna_ref, gssd_ref, gmem_ref,
                  wna_ref, wssd_ref, wmem_ref, wout_ref, o_ref):
    merged = gna_ref[...].astype(F32) * jnp.dot(ona_ref[...], wna_ref[...], preferred_element_type=F32)
    merged += gssd_ref[...].astype(F32) * jnp.dot(ossd_ref[...], wssd_ref[...], preferred_element_type=F32)
    merged += gmem_ref[...].astype(F32) * jnp.dot(omem_ref[...], wmem_ref[...], preferred_element_type=F32)
    o_ref[...] = x_ref[...] + jnp.dot(merged.astype(BF16), wout_ref[...], preferred_element_type=F32)


def _merge(x1, o_na, o_ssd, o_mem, proj, w_na, w_ssd, w_mem, w_out, tm=512):
    n = x1.shape[0]
    tok = lambda w: pl.BlockSpec((tm, w), lambda i: (i, 0))
    gate = lambda k: pl.BlockSpec((tm, PROJ_TILE), lambda i: (i, T_GATE + k))
    return pl.pallas_call(
        _merge_kernel,
        grid=(n // tm,),
        in_specs=[tok(D_MODEL), tok(D_MODEL), tok(D_INNER), tok(D_MODEL), gate(0), gate(1), gate(2),
                  _const_spec((D_MODEL, D_MODEL)), _const_spec((D_INNER, D_MODEL)),
                  _const_spec((D_MODEL, D_MODEL)), _const_spec((D_MODEL, D_MODEL))],
        out_specs=tok(D_MODEL),
        out_shape=jax.ShapeDtypeStruct((n, D_MODEL), F32),
        compiler_params=_params(("parallel",)),
        name="merge_out",
    )(x1, o_na, o_ssd, o_mem, proj, proj, proj, w_na, w_ssd, w_mem, w_out)


def _prepare_weights(ffn1_norm, ffn1_w_gate, ffn1_w_up, ffn1_w_down, mix_norm, w_in, na_q_norm, na_k_norm,
                     na_rpb, conv_w, conv_b, dt_bias_f, dt_bias_b, a_log_f, a_log_b, ssd_d, ssd_norm,
                     mem_norm, w_mem_kv, mem_q_norm, mem_k_norm, w_br_na, w_br_ssd, w_br_mem, w_out,
                     ffn2_norm, ffn2_w_gate, ffn2_w_up, ffn2_w_down):
    bf = lambda w: w.astype(BF16)
    row = lambda v: v.astype(F32).reshape(1, -1)
    dt0 = 3 * NA_HEADS * NA_HEAD_DIM + MEM_HEADS * MEM_HEAD_DIM + D_INNER + D_INNER + 2 * SSD_GROUPS * SSD_STATE
    n_dt = 2 * SSD_HEADS
    w_main = bf(jnp.concatenate([w_in[:, :dt0], w_in[:, dt0 + n_dt:]], axis=1))
    perm = np.concatenate([np.concatenate([np.arange(g * HEADS_PER_GROUP, (g + 1) * HEADS_PER_GROUP),
                                           SSD_HEADS + np.arange(g * HEADS_PER_GROUP, (g + 1) * HEADS_PER_GROUP)])
                           for g in range(SSD_GROUPS)])
    w_dt = w_in[:, dt0:dt0 + n_dt][:, perm]
    wdt_r = bf(jnp.pad(w_dt, ((0, 0), (0, LANES - n_dt)))).T
    pad_v = lambda v: jnp.pad(v.astype(F32)[perm], (0, LANES - n_dt)).reshape(1, LANES)
    dt_bias = pad_v(jnp.concatenate([dt_bias_f, dt_bias_b]))
    a_neg = pad_v(jnp.concatenate([-jnp.exp(a_log_f.astype(F32)), -jnp.exp(a_log_b.astype(F32))]))
    is_fwd = pad_v(jnp.concatenate([jnp.ones((SSD_HEADS,), F32), jnp.zeros((SSD_HEADS,), F32)]))
    gains = jnp.stack([
        jnp.tile(na_q_norm.astype(F32), NA_HEADS) * (NA_HEAD_DIM ** -0.5 * LOG2E),
        jnp.tile(na_k_norm.astype(F32), NA_HEADS),
        jnp.ones((PROJ_TILE,), F32),
        jnp.tile(mem_q_norm.astype(F32), MEM_HEADS) * (MEM_HEAD_DIM ** -0.5 * LOG2E),
    ]).reshape(4, 1, PROJ_TILE)
    blk = np.arange(256) // NA_HEAD_DIM
    nmat64 = (blk[:, None] == blk[None, :]).astype(np.float32) / NA_HEAD_DIM
    nmat256 = np.full((256, 256), 1.0 / MEM_HEAD_DIM, np.float32)
    nmats = jnp.asarray(np.stack([nmat64, nmat256]), BF16)
    return dict(
        ffn1=(row(ffn1_norm), bf(ffn1_w_gate), bf(ffn1_w_up), bf(ffn1_w_down), row(mix_norm)),
        ffn2=(row(ffn2_norm), bf(ffn2_w_gate), bf(ffn2_w_up), bf(ffn2_w_down), row(ffn2_norm)),
        inproj=(w_main, gains, nmats, wdt_r),
        dt=(dt_bias, a_neg, is_fwd),
        na_bias=_na_bias_table(na_rpb),
        conv=(conv_w.astype(F32), row(conv_b)),
        d_rep=jnp.repeat(ssd_d.astype(F32), SSD_HEAD_DIM).reshape(1, D_INNER),
        ssd_norm=row(ssd_norm),
        mem=(row(mem_norm), bf(w_mem_kv), row(mem_k_norm)),
        merge=(bf(w_br_na), bf(w_br_ssd), bf(w_br_mem), bf(w_out)),
    )


def _encoder_layer(x, mem, w):
    b, t, _ = x.shape
    n = b * t
    x1, u = _ffn(x.reshape(n, D_MODEL), *w["ffn1"], with_u=True)
    proj, dtr = _inproj(u, *w["inproj"])
    rowpack = _dtprep(dtr, *w["dt"], b, t)
    proj3 = proj.reshape(b, t, N_PROJ_TILES * PROJ_TILE)
    o_na = _na(proj3, w["na_bias"], b, t)
    o_ssd = _ssd(proj3, *w["conv"], rowpack, w["d_rep"], w["ssd_norm"], b, t)
    o_mem = _mem(proj3, mem, *w["mem"], b, t)
    x2 = _merge(x1, o_na.reshape(n, -1), o_ssd.reshape(n, -1), o_mem.reshape(n, -1), proj, *w["merge"])
    (y,) = _ffn(x2, *w["ffn2"], with_u=False)
    return y.reshape(b, t, D_MODEL)


def kernel(x_prompt, x_sample, mem_prompt, mem_sample, ffn1_norm, ffn1_w_gate, ffn1_w_up, ffn1_w_down, mix_norm, w_in, na_q_norm, na_k_norm, na_rpb, conv_w, conv_b, dt_bias_f, dt_bias_b, a_log_f, a_log_b, ssd_d, ssd_norm, mem_norm, w_mem_kv, mem_q_norm, mem_k_norm, w_br_na, w_br_ssd, w_br_mem, w_out, ffn2_norm, ffn2_w_gate, ffn2_w_up, ffn2_w_down):
    layer = (ffn1_norm, ffn1_w_gate, ffn1_w_up, ffn1_w_down, mix_norm, w_in, na_q_norm, na_k_norm, na_rpb,
             conv_w, conv_b, dt_bias_f, dt_bias_b, a_log_f, a_log_b, ssd_d, ssd_norm, mem_norm, w_mem_kv,
             mem_q_norm, mem_k_norm, w_br_na, w_br_ssd, w_br_mem, w_out, ffn2_norm, ffn2_w_gate, ffn2_w_up,
             ffn2_w_down)
    assert all(p.shape[0] == 1 for p in layer), "single-layer model"
    w = _prepare_weights(*[p[0] for p in layer])
    return (_encoder_layer(x_prompt, mem_prompt, w), _encoder_layer(x_sample, mem_sample, w))
```

```python
import functools

import jax
import jax.numpy as jnp
import numpy as np
from jax import lax
from jax.experimental import pallas as pl
from jax.experimental.pallas import tpu as pltpu

F32 = jnp.float32
BF16 = jnp.bfloat16

D_MODEL = 1024
GRID_W = 64
NA_HEADS = 16
NA_HEAD_DIM = 64
WIN_R = 8
WIN_C = 16
D_INNER = 2048
SSD_HEADS = 32
SSD_HEAD_DIM = 64
SSD_GROUPS = 8
SSD_STATE = 128
SSD_CONV = 5
CHUNK = 128
MEM_TOKENS = 256
MEM_HEADS = 4
MEM_HEAD_DIM = 256
D_FF = 2816
RMS_EPS = 1e-6
LOG2E = 1.4426950408889634
NEG_INF = -1e30

LANES = 128
HEADS_PER_GROUP = SSD_HEADS // SSD_GROUPS
GROUP_W = HEADS_PER_GROUP * SSD_HEAD_DIM
PROJ_TILE = 1024
T_Q, T_K, T_V, T_QM, T_Z, T_XS, T_BM, T_CM, T_GATE = 0, 1, 2, 3, 4, 6, 8, 9, 10
N_PROJ_TILES = 13
VMEM_LIMIT = 56 * 1024 * 1024

NT_DIMS = (((1,), (1,)), ((), ()))


def _params(sem, vmem=VMEM_LIMIT):
    return pltpu.CompilerParams(dimension_semantics=sem, vmem_limit_bytes=vmem)


def _const_spec(shape):
    nd = len(shape)
    return pl.BlockSpec(shape, lambda *_: (0,) * nd, pipeline_mode=pl.Buffered(1))


def _sigmoid(x):
    return 1.0 / (1.0 + jnp.exp(-x))


def _sigmoid_tanh(x):
    return 0.5 + 0.5 * jnp.tanh(0.5 * x)


def _silu(x):
    return x * _sigmoid(x)


def _rms(x, g):
    return x * lax.rsqrt(jnp.mean(x * x, axis=-1, keepdims=True) + RMS_EPS) * g


def _ffn_kernel(x_ref, g_ref, wg_ref, wu_ref, wd_ref, g2_ref, o_ref, *u_ref):
    x = x_ref[...]
    xn = _rms(x, g_ref[...]).astype(BF16)
    gate = jnp.dot(xn, wg_ref[...], preferred_element_type=F32)
    up = jnp.dot(xn, wu_ref[...], preferred_element_type=F32)
    h = (_silu(gate) * up).astype(BF16)
    y = x + 0.5 * jnp.dot(h, wd_ref[...], preferred_element_type=F32)
    o_ref[...] = y
    if u_ref:
        u_ref[0][...] = _rms(y, g2_ref[...]).astype(BF16)


def _ffn(x, g, wg, wu, wd, g2, with_u, tm=512):
    n = x.shape[0]
    tok = pl.BlockSpec((tm, D_MODEL), lambda i: (i, 0))
    out_shape = [jax.ShapeDtypeStruct((n, D_MODEL), F32)]
    out_specs = [tok]
    if with_u:
        out_shape.append(jax.ShapeDtypeStruct((n, D_MODEL), BF16))
        out_specs.append(tok)
    res = pl.pallas_call(
        _ffn_kernel,
        grid=(n // tm,),
        in_specs=[tok, _const_spec((1, D_MODEL)), _const_spec((D_MODEL, D_FF)),
                  _const_spec((D_MODEL, D_FF)), _const_spec((D_FF, D_MODEL)), _const_spec((1, D_MODEL))],
        out_specs=out_specs,
        out_shape=out_shape,
        compiler_params=_params(("parallel",)),
        name="ffn_u" if with_u else "ffn",
    )(x, g, wg, wu, wd, g2)
    return res


NORM_W = 256


def _inproj_kernel(u_ref, w_ref, gain_ref, nmat_ref, wdr_ref, o_ref, dtr_ref):
    j = pl.program_id(1)
    is_gate = j >= T_GATE
    is_silu = (j == T_Z) | (j == T_Z + 1)
    is_norm = (j == T_Q) | (j == T_K) | (j == T_QM)

    @pl.when(j == 0)
    def _():
        dtr_ref[...] = lax.dot_general(wdr_ref[...], u_ref[...], NT_DIMS, preferred_element_type=F32)

    def tile(epilogue):
        u = u_ref[...]
        for c in range(PROJ_TILE // NORM_W):
            cs = slice(c * NORM_W, (c + 1) * NORM_W)
            acc = jnp.dot(u, w_ref[:, cs], preferred_element_type=F32)
            o_ref[:, cs] = epilogue(acc, cs).astype(BF16)

    @pl.when(is_norm)
    def _():
        acc = jnp.dot(u_ref[...], w_ref[...], preferred_element_type=F32)
        for c in range(PROJ_TILE // NORM_W):
            cs = slice(c * NORM_W, (c + 1) * NORM_W)
            a = acc[:, cs]
            ms = jnp.dot((a * a).astype(BF16), nmat_ref[0], preferred_element_type=F32)
            o_ref[:, cs] = (a * lax.rsqrt(ms + RMS_EPS) * gain_ref[0, :, cs]).astype(BF16)

    pl.when(is_silu)(lambda: tile(lambda a, cs: a * _sigmoid_tanh(a)))
    pl.when(is_gate)(lambda: tile(lambda a, cs: _sigmoid_tanh(a)))
    pl.when(jnp.logical_not(is_norm | is_silu | is_gate))(lambda: tile(lambda a, cs: a))


def _inproj(u, w_main, gains, nmats, wdt_r, tm=2048):
    n = u.shape[0]
    return pl.pallas_call(
        _inproj_kernel,
        grid=(n // tm, N_PROJ_TILES),
        in_specs=[
            pl.BlockSpec((tm, D_MODEL), lambda i, j: (i, 0)),
            pl.BlockSpec((D_MODEL, PROJ_TILE), lambda i, j: (0, j)),
            pl.BlockSpec((1, 1, PROJ_TILE), lambda i, j: (jnp.minimum(j, T_QM), 0, 0)),
            pl.BlockSpec((1, 256, 256), lambda i, j: (jnp.where(j == T_QM, 1, 0), 0, 0)),
            pl.BlockSpec((LANES, D_MODEL), lambda i, j: (0, 0)),
        ],
        out_specs=[
            pl.BlockSpec((tm, PROJ_TILE), lambda i, j: (i, j)),
            pl.BlockSpec((LANES, tm), lambda i, j: (0, i)),
        ],
        out_shape=[
            jax.ShapeDtypeStruct((n, N_PROJ_TILES * PROJ_TILE), BF16),
            jax.ShapeDtypeStruct((LANES, n), F32),
        ],
        compiler_params=_params(("parallel", "arbitrary")),
        name="in_proj",
    )(u, w_main, gains, nmats, wdt_r)


def _softplus(x):
    return jnp.maximum(x, 0.0) + jnp.log1p(jnp.exp(-jnp.abs(x)))


def _split3(x):
    p1 = x.astype(BF16)
    r1 = x - p1.astype(F32)
    p2 = r1.astype(BF16)
    p3 = (r1 - p2.astype(F32)).astype(BF16)
    return p1, p2, p3


ROW_A, ROW_W, ROW_A2 = 0, 8, 16
ROW_PACK = 24


def _dtprep_kernel(dtr_ref, br_ref, ar_ref, fr_ref, pack_ref):
    t = dtr_ref.shape[1]
    per_g = 2 * HEADS_PER_GROUP
    li = lax.broadcasted_iota(jnp.int32, (CHUNK, CHUNK), 0)
    ui = lax.broadcasted_iota(jnp.int32, (CHUNK, CHUNK), 1)
    tri_gt = (ui > li).astype(F32).astype(BF16)
    tri_ge = (ui >= li).astype(F32).astype(BF16)
    isf_r = fr_ref[...] > 0.5
    for c in range(t // CHUNK):
        sl = slice(c * CHUNK, (c + 1) * CHUNK)
        dtr = _softplus(dtr_ref[:, sl] + br_ref[...])
        dar = dtr * ar_ref[...]
        pre_r = jnp.zeros((LANES, CHUNK), F32)
        suf_r = jnp.zeros((LANES, CHUNK), F32)
        for p in _split3(dar):
            pre_r += jnp.dot(p, tri_ge, preferred_element_type=F32)
            suf_r += lax.dot_general(p, tri_gt, NT_DIMS, preferred_element_type=F32)
        a2 = jnp.where(isf_r, pre_r, suf_r + dar) * LOG2E
        a_row = a2 - jnp.log2(dtr)
        w_row = jnp.exp(jnp.where(isf_r, suf_r, pre_r - dar)) * dtr
        for g in range(SSD_GROUPS):
            gs = slice(g * per_g, (g + 1) * per_g)
            pack_ref[0, g, c, ROW_A:ROW_A + per_g, :] = a_row[gs]
            pack_ref[0, g, c, ROW_W:ROW_W + per_g, :] = w_row[gs]
            pack_ref[0, g, c, ROW_A2:ROW_A2 + per_g, :] = a2[gs]


def _dtprep(dtr, bias_c, a_c, isf_c, b, t):
    nc = t // CHUNK
    vr = pl.BlockSpec((LANES, 1), lambda i: (0, 0))
    return pl.pallas_call(
        _dtprep_kernel,
        grid=(b,),
        in_specs=[pl.BlockSpec((LANES, t), lambda i: (0, i)), vr, vr, vr],
        out_specs=pl.BlockSpec((1, SSD_GROUPS, nc, ROW_PACK, CHUNK), lambda i: (i, 0, 0, 0, 0)),
        out_shape=jax.ShapeDtypeStruct((b, SSD_GROUPS, nc, ROW_PACK, CHUNK), F32),
        compiler_params=_params(("parallel",)),
        name="dt_prep",
    )(dtr, bias_c.reshape(LANES, 1), a_c.reshape(LANES, 1), isf_c.reshape(LANES, 1))


NA_ROWS_PER_STEP = 8
NA_KEYS = WIN_R * GRID_W
NA_PAIRS_AHEAD = 3


def _na_kernel(q_ref, k_ref, v_ref, bias_ref, o_ref):
    rows = k_ref.shape[1] // GRID_W
    rb = pl.program_id(1)
    lane = lax.broadcasted_iota(jnp.int32, (GRID_W, LANES), 1)
    lo = lane < NA_HEAD_DIM
    lane1 = lax.broadcasted_iota(jnp.int32, (1, LANES), 1)
    mask_a = (lane1 < NA_HEAD_DIM).astype(F32).astype(BF16)
    mask_b = (lane1 >= NA_HEAD_DIM).astype(F32).astype(BF16)

    def row_body(rl, carry):
        r = rb * NA_ROWS_PER_STEP + rl
        rs = jnp.clip(r - WIN_R // 2, 0, rows - WIN_R)
        s0 = (WIN_R - 1) - (r - rs)
        tbl = (s0 % 2) * 8 + s0 // 2
        q_off = pl.multiple_of(rl * GRID_W, GRID_W)
        k_off = pl.multiple_of(rs * GRID_W, GRID_W)
        def scores(p):
            cs = slice(p * LANES, (p + 1) * LANES)
            q2 = q_ref[0, pl.ds(q_off, GRID_W), cs]
            qblk = jnp.concatenate([q2 * mask_a, q2 * mask_b], axis=0)
            k2 = k_ref[0, pl.ds(k_off, NA_KEYS), cs]
            s = lax.dot_general(qblk, k2, NT_DIMS, preferred_element_type=F32)
            bias = jnp.concatenate([bias_ref[p, tbl + i] for i in range(NA_KEYS // LANES)], axis=1)
            return s + bias

        def attend(p, s):
            cs = slice(p * LANES, (p + 1) * LANES)
            v2 = v_ref[0, pl.ds(k_off, NA_KEYS), cs]
            m = jnp.max(s, axis=1, keepdims=True)
            e = jnp.exp2(s - m)
            l = jnp.sum(e, axis=1, keepdims=True)
            o2 = jnp.dot(e.astype(BF16), v2, preferred_element_type=F32) * (1.0 / l)
            o = jnp.where(lo, o2[:GRID_W], o2[GRID_W:])
            o_ref[0, pl.ds(q_off, GRID_W), cs] = o.astype(BF16)

        n_pairs = NA_HEADS // 2
        pending = [scores(p) for p in range(NA_PAIRS_AHEAD)]
        for p in range(n_pairs):
            if p + NA_PAIRS_AHEAD < n_pairs:
                pending.append(scores(p + NA_PAIRS_AHEAD))
            attend(p, pending.pop(0))
        return carry

    lax.fori_loop(0, NA_ROWS_PER_STEP, row_body, 0, unroll=2)


def _na(proj, bias_tbl, b, t):
    blk = NA_ROWS_PER_STEP * GRID_W
    return pl.pallas_call(
        _na_kernel,
        grid=(b, t // blk),
        in_specs=[
            pl.BlockSpec((1, blk, PROJ_TILE), lambda i, r: (i, r, T_Q)),
            pl.BlockSpec((1, t, PROJ_TILE), lambda i, r: (i, 0, T_K)),
            pl.BlockSpec((1, t, PROJ_TILE), lambda i, r: (i, 0, T_V)),
            _const_spec(bias_tbl.shape),
        ],
        out_specs=pl.BlockSpec((1, blk, PROJ_TILE), lambda i, r: (i, r, 0)),
        out_shape=jax.ShapeDtypeStruct((b, t, NA_HEADS * NA_HEAD_DIM), BF16),
        compiler_params=_params(("parallel", "arbitrary")),
        name="na_attn",
    )(proj, proj, proj, bias_tbl)


def _na_bias_table(rpb):
    qc = np.arange(GRID_W)[:, None]
    kc = np.arange(GRID_W)[None, :]
    cs = np.clip(qc - WIN_C // 2, 0, GRID_W - WIN_C)
    valid = (kc >= cs) & (kc < cs + WIN_C)
    side = GRID_W - WIN_C
    rp = jnp.pad(rpb.astype(F32) * LOG2E, ((0, 0), (0, 3), (side, side)))
    t1 = jnp.stack([rp[:, :, GRID_W - 1 - q:2 * GRID_W - 1 - q] for q in range(GRID_W)], axis=2)
    row_ok = (np.arange(2 * WIN_R + 2) < 2 * WIN_R - 1)[None, :, None, None]
    t1 = jnp.where(jnp.asarray(valid[None, None] & row_ok), t1, NEG_INF)
    t2 = jnp.stack([jnp.stack([t1[:, par + jj:par + jj + 16:2] for jj in (0, 1)], axis=3) for par in (0, 1)],
                   axis=1)
    t2 = t2.reshape(NA_HEADS // 2, 2, 2, 8, GRID_W, 2, GRID_W)
    t2 = jnp.transpose(t2, (0, 2, 3, 1, 4, 5, 6))
    return t2.reshape(NA_HEADS // 2, 16, 2 * GRID_W, 2 * GRID_W)


def _ssd_kernel(xs_ref, bm_ref, cm_ref, z_ref, cwx_ref, cbx_ref, cwb_ref, cbb_ref, cwc_ref, cbc_ref,
                row_ref, d_ref, nrm_ref, o_ref,
                conv_in_s, conv_out_s, xm_s, bm_s, cm_s, bmt_s, g_s, y_s):
    t = xs_ref.shape[1]
    nc = t // CHUNK
    n = SSD_STATE
    half = SSD_CONV // 2
    n_pairs = HEADS_PER_GROUP // 2

    seg = conv_in_s.shape[1] // 8 - 2
    slabs = [(xs_ref, 0, cwx_ref, cbx_ref), (xs_ref, LANES, cwx_ref, cbx_ref),
             (bm_ref, 0, cwb_ref, cbb_ref), (cm_ref, 0, cwc_ref, cbc_ref)]
    taps, biases = [], []
    for s, (src_ref, l0, w_ref, b_ref) in enumerate(slabs):
        conv_in_s[s, 0:8, :] = jnp.zeros((8, LANES), F32)
        conv_in_s[s, 8 + t:, :] = jnp.zeros((conv_in_s.shape[1] - 8 - t, LANES), F32)
        conv_in_s[s, 8:8 + t, :] = src_ref[0, :, l0:l0 + LANES].astype(F32)
        taps.append([jnp.broadcast_to(0.5 * w_ref[k:k + 1, l0:l0 + LANES], (8, LANES)) for k in range(SSD_CONV)])
        biases.append(jnp.broadcast_to(0.5 * b_ref[:, l0:l0 + LANES], (8, LANES)))

    def conv_body(j, carry):
        for s in range(len(slabs)):
            acc = biases[s]
            for k in range(SSD_CONV):
                acc = acc + taps[s][k] * conv_in_s[s, pl.ds(8 - half + k + j, 8, stride=seg), :]
            conv_out_s[s, pl.ds(j, 8, stride=seg), :] = acc + acc * jnp.tanh(acc)
        return carry

    lax.fori_loop(0, seg, conv_body, 0, unroll=10)

    lo3 = lax.broadcasted_iota(jnp.int32, (nc, CHUNK, LANES), 2) < SSD_HEAD_DIM
    for p in range(n_pairs):
        xp = conv_out_s[p, 0:t, :].reshape(nc, CHUNK, LANES)
        y_s[:, :, p * LANES:(p + 1) * LANES] = xp * d_ref[:, p * LANES:(p + 1) * LANES]
        xm_s[:, p, 0:CHUNK, :] = jnp.where(lo3, xp, 0.0).astype(BF16)
        xm_s[:, p, CHUNK:2 * CHUNK, :] = jnp.where(lo3, 0.0, xp).astype(BF16)
    bm_s[...] = conv_out_s[2, 0:t, :].astype(BF16).reshape(nc, CHUNK, n)
    cm_s[...] = conv_out_s[3, 0:t, :].astype(BF16).reshape(nc, CHUNK, n)

    ri = lax.broadcasted_iota(jnp.int32, (n, n), 0)
    ci = lax.broadcasted_iota(jnp.int32, (n, n), 1)
    eye = (ri == ci).astype(F32).astype(BF16)
    for c in range(nc):
        bmt_s[c] = lax.dot_general(eye, bm_s[c], NT_DIMS, preferred_element_type=F32).astype(BF16)

    lower = ci <= ri
    upper = ci >= ri
    lo = lax.broadcasted_iota(jnp.int32, (CHUNK, LANES), 1) < SSD_HEAD_DIM
    lo1 = lax.broadcasted_iota(jnp.int32, (1, LANES), 1) < SSD_HEAD_DIM

    for c in range(nc):
        g_s[c] = lax.dot_general(cm_s[c], bm_s[c], NT_DIMS, preferred_element_type=F32)

    def chunk_step(c, direction, h, g):
        mask = lower if direction == 0 else upper
        off = direction * HEADS_PER_GROUP
        bt = bmt_s[c]
        ch = jnp.dot(cm_s[c], h.astype(BF16), preferred_element_type=F32)
        rows = row_ref[0, 0, c]
        edge = CHUNK - 1 if direction == 0 else 0
        dstate, decay_h = [], []
        for p in range(n_pairs):
            m_parts, e_parts, b_parts = [], [], []
            for hh in (2 * p, 2 * p + 1):
                k = off + hh
                a_col = jnp.broadcast_to(rows[ROW_A2 + k:ROW_A2 + k + 1, :], (CHUNK, CHUNK)).T
                a_row = rows[ROW_A + k:ROW_A + k + 1, :]
                w_row = rows[ROW_W + k:ROW_W + k + 1, :]
                decay = jnp.exp2(jnp.where(mask, a_col - a_row, -jnp.inf))
                m_parts.append((g * decay).astype(BF16))
                e_parts.append(jnp.exp2(a_col))
                b_parts.append(bt * w_row.astype(BF16))
            xm = xm_s[c, p]
            y_in = jnp.dot(jnp.concatenate(m_parts, axis=1), xm, preferred_element_type=F32)
            y_off = ch[:, p * LANES:(p + 1) * LANES] * jnp.where(lo, e_parts[0], e_parts[1])
            y_s[c, :, p * LANES:(p + 1) * LANES] += y_in + y_off
            dstate.append(jnp.dot(jnp.concatenate(b_parts, axis=1), xm, preferred_element_type=F32))
            decay_h.append(jnp.where(lo1, e_parts[0][edge:edge + 1, :], e_parts[1][edge:edge + 1, :]))
        return h * jnp.concatenate(decay_h, axis=1) + jnp.concatenate(dstate, axis=1)

    def body(it, carry):
        h_f, h_b = carry
        return chunk_step(it, 0, h_f, g_s[it]), chunk_step(nc - 1 - it, 1, h_b, g_s[nc - 1 - it])

    h0 = jnp.zeros((n, GROUP_W), F32)
    lax.fori_loop(0, nc, body, (h0, h0), unroll=8)

    y = y_s[...].reshape(t, GROUP_W) * z_ref[0].astype(F32)
    o_ref[0] = _rms(y, nrm_ref[...]).astype(BF16)


def _ssd(proj, conv_w, conv_b, rowpack, d_rep, ssd_norm, b, t):
    nc = t // CHUNK
    xs_tile0 = T_XS * PROJ_TILE // GROUP_W
    z_tile0 = T_Z * PROJ_TILE // GROUP_W
    bm_tile0 = T_BM * PROJ_TILE // SSD_STATE
    cm_tile0 = T_CM * PROJ_TILE // SSD_STATE
    cb0 = D_INNER // SSD_STATE
    cc0 = cb0 + SSD_GROUPS
    seg = t // 8 + (4 - t // 8) % 8
    return pl.pallas_call(
        _ssd_kernel,
        grid=(b, SSD_GROUPS),
        in_specs=[
            pl.BlockSpec((1, t, GROUP_W), lambda i, g: (i, 0, xs_tile0 + g)),
            pl.BlockSpec((1, t, SSD_STATE), lambda i, g: (i, 0, bm_tile0 + g)),
            pl.BlockSpec((1, t, SSD_STATE), lambda i, g: (i, 0, cm_tile0 + g)),
            pl.BlockSpec((1, t, GROUP_W), lambda i, g: (i, 0, z_tile0 + g)),
            pl.BlockSpec((SSD_CONV, GROUP_W), lambda i, g: (0, g)),
            pl.BlockSpec((1, GROUP_W), lambda i, g: (0, g)),
            pl.BlockSpec((SSD_CONV, SSD_STATE), lambda i, g: (0, cb0 + g)),
            pl.BlockSpec((1, SSD_STATE), lambda i, g: (0, cb0 + g)),
            pl.BlockSpec((SSD_CONV, SSD_STATE), lambda i, g: (0, cc0 + g)),
            pl.BlockSpec((1, SSD_STATE), lambda i, g: (0, cc0 + g)),
            pl.BlockSpec((1, 1, nc, ROW_PACK, CHUNK), lambda i, g: (i, g, 0, 0, 0)),
            pl.BlockSpec((1, GROUP_W), lambda i, g: (0, g)),
            pl.BlockSpec((1, GROUP_W), lambda i, g: (0, g)),
        ],
        out_specs=pl.BlockSpec((1, t, GROUP_W), lambda i, g: (i, 0, g)),
        out_shape=jax.ShapeDtypeStruct((b, t, D_INNER), BF16),
        scratch_shapes=[
            pltpu.VMEM((4, 8 * (seg + 2), LANES), F32),
            pltpu.VMEM((4, 8 * seg, LANES), F32),
            pltpu.VMEM((nc, HEADS_PER_GROUP // 2, 2 * CHUNK, LANES), BF16),
            pltpu.VMEM((nc, CHUNK, SSD_STATE), BF16),
            pltpu.VMEM((nc, CHUNK, SSD_STATE), BF16),
            pltpu.VMEM((nc, SSD_STATE, CHUNK), BF16),
            pltpu.VMEM((nc, CHUNK, CHUNK), F32),
            pltpu.VMEM((nc, CHUNK, GROUP_W), F32),
        ],
        compiler_params=_params(("parallel", "arbitrary")),
        name="ssd",
    )(proj, proj, proj, proj, conv_w, conv_b, conv_w, conv_b, conv_w, conv_b,
      rowpack, d_rep, ssd_norm)


def _mem_kernel(q_ref, mem_ref, gm_ref, wkv_ref, gk_ref, o_ref, k_s, v_s):
    width = MEM_HEADS * MEM_HEAD_DIM

    @pl.when(pl.program_id(1) == 0)
    def _():
        mn = _rms(mem_ref[0], gm_ref[...]).astype(BF16)
        kv = jnp.dot(mn, wkv_ref[...], preferred_element_type=F32)
        for h in range(MEM_HEADS):
            hs = slice(h * MEM_HEAD_DIM, (h + 1) * MEM_HEAD_DIM)
            k_s[:, hs] = _rms(kv[:, hs], gk_ref[...]).astype(BF16)
        v_s[...] = kv[:, width:].astype(BF16)

    for h in range(MEM_HEADS):
        hs = slice(h * MEM_HEAD_DIM, (h + 1) * MEM_HEAD_DIM)
        s = lax.dot_general(q_ref[0, :, hs], k_s[:, hs], NT_DIMS, preferred_element_type=F32)
        m = jnp.max(s, axis=1, keepdims=True)
        e = jnp.exp2(s - m)
        l = jnp.sum(e, axis=1, keepdims=True)
        o = jnp.dot(e.astype(BF16), v_s[:, hs], preferred_element_type=F32) * (1.0 / l)
        o_ref[0, :, hs] = o.astype(BF16)


def _mem(proj, mem, g_mem, w_kv, g_k, b, t, tq=512):
    width = MEM_HEADS * MEM_HEAD_DIM
    return pl.pallas_call(
        _mem_kernel,
        grid=(b, t // tq),
        in_specs=[
            pl.BlockSpec((1, tq, PROJ_TILE), lambda i, j: (i, j, T_QM)),
            pl.BlockSpec((1, MEM_TOKENS, D_MODEL), lambda i, j: (i, 0, 0)),
            _const_spec((1, D_MODEL)),
            _const_spec((D_MODEL, 2 * width)),
            _const_spec((1, MEM_HEAD_DIM)),
        ],
        out_specs=pl.BlockSpec((1, tq, width), lambda i, j: (i, j, 0)),
        out_shape=jax.ShapeDtypeStruct((b, t, width), BF16),
        scratch_shapes=[pltpu.VMEM((MEM_TOKENS, width), BF16), pltpu.VMEM((MEM_TOKENS, width), BF16)],
        compiler_params=_params(("parallel", "arbitrary")),
        name="mem_attn",
    )(proj, mem, g_mem, w_kv, g_k)


def _merge_kernel(x_ref, ona_ref, ossd_ref, omem_ref, gna_ref, gssd_ref, gmem_ref,
                  wna_ref, wssd_ref, wmem_ref, wout_ref, o_ref):
    merged = gna_ref[...].astype(F32) * jnp.dot(ona_ref[...], wna_ref[...], preferred_element_type=F32)
    merged += gssd_ref[...].astype(F32) * jnp.dot(ossd_ref[...], wssd_ref[...], preferred_element_type=F32)
    merged += gmem_ref[...].astype(F32) * jnp.dot(omem_ref[...], wmem_ref[...], preferred_element_type=F32)
    o_ref[...] = x_ref[...] + jnp.dot(merged.astype(BF16), wout_ref[...], preferred_element_type=F32)


def _merge(x1, o_na, o_ssd, o_mem, proj, w_na, w_ssd, w_mem, w_out, tm=512):
    n = x1.shape[0]
    tok = lambda w: pl.BlockSpec((tm, w), lambda i: (i, 0))
    gate = lambda k: pl.BlockSpec((tm, PROJ_TILE), lambda i: (i, T_GATE + k))
    return pl.pallas_call(
        _merge_kernel,
        grid=(n // tm,),
        in_specs=[tok(D_MODEL), tok(D_MODEL), tok(D_INNER), tok(D_MODEL), gate(0), gate(1), gate(2),
                  _const_spec((D_MODEL, D_MODEL)), _const_spec((D_INNER, D_MODEL)),
                  _const_spec((D_MODEL, D_MODEL)), _const_spec((D_MODEL, D_MODEL))],
        out_specs=tok(D_MODEL),
        out_shape=jax.ShapeDtypeStruct((n, D_MODEL), F32),
        compiler_params=_params(("parallel",)),
        name="merge_out",
    )(x1, o_na, o_ssd, o_mem, proj, proj, proj, w_na, w_ssd, w_mem, w_out)


def _prepare_weights(ffn1_norm, ffn1_w_gate, ffn1_w_up, ffn1_w_down, mix_norm, w_in, na_q_norm, na_k_norm,
                     na_rpb, conv_w, conv_b, dt_bias_f, dt_bias_b, a_log_f, a_log_b, ssd_d, ssd_norm,
                     mem_norm, w_mem_kv, mem_q_norm, mem_k_norm, w_br_na, w_br_ssd, w_br_mem, w_out,
                     ffn2_norm, ffn2_w_gate, ffn2_w_up, ffn2_w_down):
    bf = lambda w: w.astype(BF16)
    row = lambda v: v.astype(F32).reshape(1, -1)
    dt0 = 3 * NA_HEADS * NA_HEAD_DIM + MEM_HEADS * MEM_HEAD_DIM + D_INNER + D_INNER + 2 * SSD_GROUPS * SSD_STATE
    n_dt = 2 * SSD_HEADS
    w_main = bf(jnp.concatenate([w_in[:, :dt0], w_in[:, dt0 + n_dt:]], axis=1))
    perm = np.concatenate([np.concatenate([np.arange(g * HEADS_PER_GROUP, (g + 1) * HEADS_PER_GROUP),
                                           SSD_HEADS + np.arange(g * HEADS_PER_GROUP, (g + 1) * HEADS_PER_GROUP)])
                           for g in range(SSD_GROUPS)])
    w_dt = w_in[:, dt0:dt0 + n_dt][:, perm]
    wdt_r = bf(jnp.pad(w_dt, ((0, 0), (0, LANES - n_dt)))).T
    pad_v = lambda v: jnp.pad(v.astype(F32)[perm], (0, LANES - n_dt)).reshape(1, LANES)
    dt_bias = pad_v(jnp.concatenate([dt_bias_f, dt_bias_b]))
    a_neg = pad_v(jnp.concatenate([-jnp.exp(a_log_f.astype(F32)), -jnp.exp(a_log_b.astype(F32))]))
    is_fwd = pad_v(jnp.concatenate([jnp.ones((SSD_HEADS,), F32), jnp.zeros((SSD_HEADS,), F32)]))
    gains = jnp.stack([
        jnp.tile(na_q_norm.astype(F32), NA_HEADS) * (NA_HEAD_DIM ** -0.5 * LOG2E),
        jnp.tile(na_k_norm.astype(F32), NA_HEADS),
        jnp.ones((PROJ_TILE,), F32),
        jnp.tile(mem_q_norm.astype(F32), MEM_HEADS) * (MEM_HEAD_DIM ** -0.5 * LOG2E),
    ]).reshape(4, 1, PROJ_TILE)
    blk = np.arange(256) // NA_HEAD_DIM
    nmat64 = (blk[:, None] == blk[None, :]).astype(np.float32) / NA_HEAD_DIM
    nmat256 = np.full((256, 256), 1.0 / MEM_HEAD_DIM, np.float32)
    nmats = jnp.asarray(np.stack([nmat64, nmat256]), BF16)
    return dict(
        ffn1=(row(ffn1_norm), bf(ffn1_w_gate), bf(ffn1_w_up), bf(ffn1_w_down), row(mix_norm)),
        ffn2=(row(ffn2_norm), bf(ffn2_w_gate), bf(ffn2_w_up), bf(ffn2_w_down), row(ffn2_norm)),
        inproj=(w_main, gains, nmats, wdt_r),
        dt=(dt_bias, a_neg, is_fwd),
        na_bias=_na_bias_table(na_rpb),
        conv=(conv_w.astype(F32), row(conv_b)),
        d_rep=jnp.repeat(ssd_d.astype(F32), SSD_HEAD_DIM).reshape(1, D_INNER),
        ssd_norm=row(ssd_norm),
        mem=(row(mem_norm), bf(w_mem_kv), row(mem_k_norm)),
        merge=(bf(w_br_na), bf(w_br_ssd), bf(w_br_mem), bf(w_out)),
    )


def _encoder_layer(x, mem, w):
    b, t, _ = x.shape
    n = b * t
    x1, u = _ffn(x.reshape(n, D_MODEL), *w["ffn1"], with_u=True)
    proj, dtr = _inproj(u, *w["inproj"])
    rowpack = _dtprep(dtr, *w["dt"], b, t)
    proj3 = proj.reshape(b, t, N_PROJ_TILES * PROJ_TILE)
    o_na = _na(proj3, w["na_bias"], b, t)
    o_ssd = _ssd(proj3, *w["conv"], rowpack, w["d_rep"], w["ssd_norm"], b, t)
    o_mem = _mem(proj3, mem, *w["mem"], b, t)
    x2 = _merge(x1, o_na.reshape(n, -1), o_ssd.reshape(n, -1), o_mem.reshape(n, -1), proj, *w["merge"])
    (y,) = _ffn(x2, *w["ffn2"], with_u=False)
    return y.reshape(b, t, D_MODEL)


def kernel(x_prompt, x_sample, mem_prompt, mem_sample, ffn1_norm, ffn1_w_gate, ffn1_w_up, ffn1_w_down, mix_norm, w_in, na_q_norm, na_k_norm, na_rpb, conv_w, conv_b, dt_bias_f, dt_bias_b, a_log_f, a_log_b, ssd_d, ssd_norm, mem_norm, w_mem_kv, mem_q_norm, mem_k_norm, w_br_na, w_br_ssd, w_br_mem, w_out, ffn2_norm, ffn2_w_gate, ffn2_w_up, ffn2_w_down):
    layer = (ffn1_norm, ffn1_w_gate, ffn1_w_up, ffn1_w_down, mix_norm, w_in, na_q_norm, na_k_norm, na_rpb,
             conv_w, conv_b, dt_bias_f, dt_bias_b, a_log_f, a_log_b, ssd_d, ssd_norm, mem_norm, w_mem_kv,
             mem_q_norm, mem_k_norm, w_br_na, w_br_ssd, w_br_mem, w_out, ffn2_norm, ffn2_w_gate, ffn2_w_up,
             ffn2_w_down)
    assert all(p.shape[0] == 1 for p in layer), "single-layer model"
    w = _prepare_weights(*[p[0] for p in layer])
    return (_encoder_layer(x_prompt, mem_prompt, w), _encoder_layer(x_sample, mem_sample, w))
```

```python
import functools

import jax
import jax.numpy as jnp
import numpy as np
from jax import lax
from jax.experimental import pallas as pl
from jax.experimental.pallas import tpu as pltpu

F32 = jnp.float32
BF16 = jnp.bfloat16

D_MODEL = 1024
GRID_W = 64
NA_HEADS = 16
NA_HEAD_DIM = 64
WIN_R = 8
WIN_C = 16
D_INNER = 2048
SSD_HEADS = 32
SSD_HEAD_DIM = 64
SSD_GROUPS = 8
SSD_STATE = 128
SSD_CONV = 5
CHUNK = 128
MEM_TOKENS = 256
MEM_HEADS = 4
MEM_HEAD_DIM = 256
D_FF = 2816
RMS_EPS = 1e-6
LOG2E = 1.4426950408889634
NEG_INF = -1e30

LANES = 128
HEADS_PER_GROUP = SSD_HEADS // SSD_GROUPS
GROUP_W = HEADS_PER_GROUP * SSD_HEAD_DIM
PROJ_TILE = 1024
T_Q, T_K, T_V, T_QM, T_Z, T_XS, T_BM, T_CM, T_GATE = 0, 1, 2, 3, 4, 6, 8, 9, 10
N_PROJ_TILES = 13
VMEM_LIMIT = 56 * 1024 * 1024

NT_DIMS = (((1,), (1,)), ((), ()))


def _params(sem, vmem=VMEM_LIMIT):
    return pltpu.CompilerParams(dimension_semantics=sem, vmem_limit_bytes=vmem)


def _const_spec(shape):
    nd = len(shape)
    return pl.BlockSpec(shape, lambda *_: (0,) * nd, pipeline_mode=pl.Buffered(1))


def _sigmoid(x):
    return 1.0 / (1.0 + jnp.exp(-x))


def _sigmoid_tanh(x):
    return 0.5 + 0.5 * jnp.tanh(0.5 * x)


def _silu(x):
    return x * _sigmoid(x)


def _rms(x, g):
    return x * lax.rsqrt(jnp.mean(x * x, axis=-1, keepdims=True) + RMS_EPS) * g


def _ffn_kernel(x_ref, g_ref, wg_ref, wu_ref, wd_ref, g2_ref, o_ref, *u_ref):
    x = x_ref[...]
    xn = _rms(x, g_ref[...]).astype(BF16)
    gate = jnp.dot(xn, wg_ref[...], preferred_element_type=F32)
    up = jnp.dot(xn, wu_ref[...], preferred_element_type=F32)
    h = (_silu(gate) * up).astype(BF16)
    y = x + 0.5 * jnp.dot(h, wd_ref[...], preferred_element_type=F32)
    o_ref[...] = y
    if u_ref:
        u_ref[0][...] = _rms(y, g2_ref[...]).astype(BF16)


def _ffn(x, g, wg, wu, wd, g2, with_u, tm=512):
    n = x.shape[0]
    tok = pl.BlockSpec((tm, D_MODEL), lambda i: (i, 0))
    out_shape = [jax.ShapeDtypeStruct((n, D_MODEL), F32)]
    out_specs = [tok]
    if with_u:
        out_shape.append(jax.ShapeDtypeStruct((n, D_MODEL), BF16))
        out_specs.append(tok)
    res = pl.pallas_call(
        _ffn_kernel,
        grid=(n // tm,),
        in_specs=[tok, _const_spec((1, D_MODEL)), _const_spec((D_MODEL, D_FF)),
                  _const_spec((D_MODEL, D_FF)), _const_spec((D_FF, D_MODEL)), _const_spec((1, D_MODEL))],
        out_specs=out_specs,
        out_shape=out_shape,
        compiler_params=_params(("parallel",)),
        name="ffn_u" if with_u else "ffn",
    )(x, g, wg, wu, wd, g2)
    return res


NORM_W = 256


def _inproj_kernel(u_ref, w_ref, gain_ref, nmat_ref, wdr_ref, o_ref, dtr_ref):
    j = pl.program_id(1)
    is_gate = j >= T_GATE
    is_silu = (j == T_Z) | (j == T_Z + 1)
    is_norm = (j == T_Q) | (j == T_K) | (j == T_QM)

    @pl.when(j == 0)
    def _():
        dtr_ref[...] = lax.dot_general(wdr_ref[...], u_ref[...], NT_DIMS, preferred_element_type=F32)

    def tile(epilogue):
        u = u_ref[...]
        for c in range(PROJ_TILE // NORM_W):
            cs = slice(c * NORM_W, (c + 1) * NORM_W)
            acc = jnp.dot(u, w_ref[:, cs], preferred_element_type=F32)
            o_ref[:, cs] = epilogue(acc, cs).astype(BF16)

    @pl.when(is_norm)
    def _():
        acc = jnp.dot(u_ref[...], w_ref[...], preferred_element_type=F32)
        for c in range(PROJ_TILE // NORM_W):
            cs = slice(c * NORM_W, (c + 1) * NORM_W)
            a = acc[:, cs]
            ms = jnp.dot((a * a).astype(BF16), nmat_ref[0], preferred_element_type=F32)
            o_ref[:, cs] = (a * lax.rsqrt(ms + RMS_EPS) * gain_ref[0, :, cs]).astype(BF16)

    pl.when(is_silu)(lambda: tile(lambda a, cs: a * _sigmoid_tanh(a)))
    pl.when(is_gate)(lambda: tile(lambda a, cs: _sigmoid_tanh(a)))
    pl.when(jnp.logical_not(is_norm | is_silu | is_gate))(lambda: tile(lambda a, cs: a))


def _inproj(u, w_main, gains, nmats, wdt_r, tm=2048):
    n = u.shape[0]
    return pl.pallas_call(
        _inproj_kernel,
        grid=(n // tm, N_PROJ_TILES),
        in_specs=[
            pl.BlockSpec((tm, D_MODEL), lambda i, j: (i, 0)),
            pl.BlockSpec((D_MODEL, PROJ_TILE), lambda i, j: (0, j)),
            pl.BlockSpec((1, 1, PROJ_TILE), lambda i, j: (jnp.minimum(j, T_QM), 0, 0)),
            pl.BlockSpec((1, 256, 256), lambda i, j: (jnp.where(j == T_QM, 1, 0), 0, 0)),
            pl.BlockSpec((LANES, D_MODEL), lambda i, j: (0, 0)),
        ],
        out_specs=[
            pl.BlockSpec((tm, PROJ_TILE), lambda i, j: (i, j)),
            pl.BlockSpec((LANES, tm), lambda i, j: (0, i)),
        ],
        out_shape=[
            jax.ShapeDtypeStruct((n, N_PROJ_TILES * PROJ_TILE), BF16),
            jax.ShapeDtypeStruct((LANES, n), F32),
        ],
        compiler_params=_params(("parallel", "arbitrary")),
        name="in_proj",
    )(u, w_main, gains, nmats, wdt_r)


def _softplus(x):
    return jnp.maximum(x, 0.0) + jnp.log1p(jnp.exp(-jnp.abs(x)))


def _split3(x):
    p1 = x.astype(BF16)
    r1 = x - p1.astype(F32)
    p2 = r1.astype(BF16)
    p3 = (r1 - p2.astype(F32)).astype(BF16)
    return p1, p2, p3


ROW_A, ROW_W, ROW_A2 = 0, 8, 16
ROW_PACK = 24


def _dtprep_kernel(dtr_ref, br_ref, ar_ref, fr_ref, pack_ref):
    t = dtr_ref.shape[1]
    per_g = 2 * HEADS_PER_GROUP
    li = lax.broadcasted_iota(jnp.int32, (CHUNK, CHUNK), 0)
    ui = lax.broadcasted_iota(jnp.int32, (CHUNK, CHUNK), 1)
    tri_gt = (ui > li).astype(F32).astype(BF16)
    tri_ge = (ui >= li).astype(F32).astype(BF16)
    isf_r = fr_ref[...] > 0.5
    for c in range(t // CHUNK):
        sl = slice(c * CHUNK, (c + 1) * CHUNK)
        dtr = _softplus(dtr_ref[:, sl] + br_ref[...])
        dar = dtr * ar_ref[...]
        pre_r = jnp.zeros((LANES, CHUNK), F32)
        suf_r = jnp.zeros((LANES, CHUNK), F32)
        for p in _split3(dar):
            pre_r += jnp.dot(p, tri_ge, preferred_element_type=F32)
            suf_r += lax.dot_general(p, tri_gt, NT_DIMS, preferred_element_type=F32)
        a2 = jnp.where(isf_r, pre_r, suf_r + dar) * LOG2E
        a_row = a2 - jnp.log2(dtr)
        w_row = jnp.exp(jnp.where(isf_r, suf_r, pre_r - dar)) * dtr
        for g in range(SSD_GROUPS):
            gs = slice(g * per_g, (g + 1) * per_g)
            pack_ref[0, g, c, ROW_A:ROW_A + per_g, :] = a_row[gs]
            pack_ref[0, g, c, ROW_W:ROW_W + per_g, :] = w_row[gs]
            pack_ref[0, g, c, ROW_A2:ROW_A2 + per_g, :] = a2[gs]


def _dtprep(dtr, bias_c, a_c, isf_c, b, t):
    nc = t // CHUNK
    vr = pl.BlockSpec((LANES, 1), lambda i: (0, 0))
    return pl.pallas_call(
        _dtprep_kernel,
        grid=(b,),
        in_specs=[pl.BlockSpec((LANES, t), lambda i: (0, i)), vr, vr, vr],
        out_specs=pl.BlockSpec((1, SSD_GROUPS, nc, ROW_PACK, CHUNK), lambda i: (i, 0, 0, 0, 0)),
        out_shape=jax.ShapeDtypeStruct((b, SSD_GROUPS, nc, ROW_PACK, CHUNK), F32),
        compiler_params=_params(("parallel",)),
        name="dt_prep",
    )(dtr, bias_c.reshape(LANES, 1), a_c.reshape(LANES, 1), isf_c.reshape(LANES, 1))


NA_ROWS_PER_STEP = 8
NA_KEYS = WIN_R * GRID_W
NA_PAIRS_AHEAD = 3


def _na_kernel(q_ref, k_ref, v_ref, bias_ref, o_ref):
    rows = k_ref.shape[1] // GRID_W
    rb = pl.program_id(1)
    lane = lax.broadcasted_iota(jnp.int32, (GRID_W, LANES), 1)
    lo = lane < NA_HEAD_DIM
    lane1 = lax.broadcasted_iota(jnp.int32, (1, LANES), 1)
    mask_a = (lane1 < NA_HEAD_DIM).astype(F32).astype(BF16)
    mask_b = (lane1 >= NA_HEAD_DIM).astype(F32).astype(BF16)

    def row_body(rl, carry):
        r = rb * NA_ROWS_PER_STEP + rl
        rs = jnp.clip(r - WIN_R // 2, 0, rows - WIN_R)
        s0 = (WIN_R - 1) - (r - rs)
        tbl = (s0 % 2) * 8 + s0 // 2
        q_off = pl.multiple_of(rl * GRID_W, GRID_W)
        k_off = pl.multiple_of(rs * GRID_W, GRID_W)
        def scores(p):
            cs = slice(p * LANES, (p + 1) * LANES)
            q2 = q_ref[0, pl.ds(q_off, GRID_W), cs]
            qblk = jnp.concatenate([q2 * mask_a, q2 * mask_b], axis=0)
            k2 = k_ref[0, pl.ds(k_off, NA_KEYS), cs]
            s = lax.dot_general(qblk, k2, NT_DIMS, preferred_element_type=F32)
            bias = jnp.concatenate([bias_ref[p, tbl + i] for i in range(NA_KEYS // LANES)], axis=1)
            return s + bias

        def attend(p, s):
            cs = slice(p * LANES, (p + 1) * LANES)
            v2 = v_ref[0, pl.ds(k_off, NA_KEYS), cs]
            m = jnp.max(s, axis=1, keepdims=True)
            e = jnp.exp2(s - m)
            l = jnp.sum(e, axis=1, keepdims=True)
            o2 = jnp.dot(e.astype(BF16), v2, preferred_element_type=F32) * (1.0 / l)
            o = jnp.where(lo, o2[:GRID_W], o2[GRID_W:])
            o_ref[0, pl.ds(q_off, GRID_W), cs] = o.astype(BF16)

        n_pairs = NA_HEADS // 2
        pending = [scores(p) for p in range(NA_PAIRS_AHEAD)]
        for p in range(n_pairs):
            if p + NA_PAIRS_AHEAD < n_pairs:
                pending.append(scores(p + NA_PAIRS_AHEAD))
            attend(p, pending.pop(0))
        return carry

    lax.fori_loop(0, NA_ROWS_PER_STEP, row_body, 0, unroll=2)


def _na(proj, bias_tbl, b, t):
    blk = NA_ROWS_PER_STEP * GRID_W
    return pl.pallas_call(
        _na_kernel,
        grid=(b, t // blk),
        in_specs=[
            pl.BlockSpec((1, blk, PROJ_TILE), lambda i, r: (i, r, T_Q)),
            pl.BlockSpec((1, t, PROJ_TILE), lambda i, r: (i, 0, T_K)),
            pl.BlockSpec((1, t, PROJ_TILE), lambda i, r: (i, 0, T_V)),
            _const_spec(bias_tbl.shape),
        ],
        out_specs=pl.BlockSpec((1, blk, PROJ_TILE), lambda i, r: (i, r, 0)),
        out_shape=jax.ShapeDtypeStruct((b, t, NA_HEADS * NA_HEAD_DIM), BF16),
        compiler_params=_params(("parallel", "arbitrary")),
        name="na_attn",
    )(proj, proj, proj, bias_tbl)


def _na_bias_table(rpb):
    qc = np.arange(GRID_W)[:, None]
    kc = np.arange(GRID_W)[None, :]
    cs = np.clip(qc - WIN_C // 2, 0, GRID_W - WIN_C)
    valid = (kc >= cs) & (kc < cs + WIN_C)
    side = GRID_W - WIN_C
    rp = jnp.pad(rpb.astype(F32) * LOG2E, ((0, 0), (0, 3), (side, side)))
    t1 = jnp.stack([rp[:, :, GRID_W - 1 - q:2 * GRID_W - 1 - q] for q in range(GRID_W)], axis=2)
    row_ok = (np.arange(2 * WIN_R + 2) < 2 * WIN_R - 1)[None, :, None, None]
    t1 = jnp.where(jnp.asarray(valid[None, None] & row_ok), t1, NEG_INF)
    t2 = jnp.stack([jnp.stack([t1[:, par + jj:par + jj + 16:2] for jj in (0, 1)], axis=3) for par in (0, 1)],
                   axis=1)
    t2 = t2.reshape(NA_HEADS // 2, 2, 2, 8, GRID_W, 2, GRID_W)
    t2 = jnp.transpose(t2, (0, 2, 3, 1, 4, 5, 6))
    return t2.reshape(NA_HEADS // 2, 16, 2 * GRID_W, 2 * GRID_W)


def _ssd_kernel(xs_ref, bm_ref, cm_ref, z_ref, cwx_ref, cbx_ref, cwb_ref, cbb_ref, cwc_ref, cbc_ref,
                row_ref, d_ref, nrm_ref, o_ref,
                conv_in_s, conv_out_s, xm_s, bm_s, cm_s, bmt_s, g_s, y_s):
    t = xs_ref.shape[1]
    nc = t // CHUNK
    n = SSD_STATE
    half = SSD_CONV // 2
    n_pairs = HEADS_PER_GROUP // 2

    seg = conv_in_s.shape[1] // 8 - 2
    slabs = [(xs_ref, 0, cwx_ref, cbx_ref), (xs_ref, LANES, cwx_ref, cbx_ref),
             (bm_ref, 0, cwb_ref, cbb_ref), (cm_ref, 0, cwc_ref, cbc_ref)]
    taps, biases = [], []
    for s, (src_ref, l0, w_ref, b_ref) in enumerate(slabs):
        conv_in_s[s, 0:8, :] = jnp.zeros((8, LANES), F32)
        conv_in_s[s, 8 + t:, :] = jnp.zeros((conv_in_s.shape[1] - 8 - t, LANES), F32)
        conv_in_s[s, 8:8 + t, :] = src_ref[0, :, l0:l0 + LANES].astype(F32)
        taps.append([jnp.broadcast_to(0.5 * w_ref[k:k + 1, l0:l0 + LANES], (8, LANES)) for k in range(SSD_CONV)])
        biases.append(jnp.broadcast_to(0.5 * b_ref[:, l0:l0 + LANES], (8, LANES)))

    def conv_body(j, carry):
        for s in range(len(slabs)):
            acc = biases[s]
            for k in range(SSD_CONV):
                acc = acc + taps[s][k] * conv_in_s[s, pl.ds(8 - half + k + j, 8, stride=seg), :]
            conv_out_s[s, pl.ds(j, 8, stride=seg), :] = acc + acc * jnp.tanh(acc)
        return carry

    lax.fori_loop(0, seg, conv_body, 0, unroll=10)

    lane3 = lax.broadcasted_iota(jnp.int32, (1, 1, LANES), 2)
    mask3_a = (lane3 < SSD_HEAD_DIM).astype(F32).astype(BF16)
    mask3_b = (lane3 >= SSD_HEAD_DIM).astype(F32).astype(BF16)
    for p in range(n_pairs):
        xp = conv_out_s[p, 0:t, :].reshape(nc, CHUNK, LANES)
        y_s[:, :, p * LANES:(p + 1) * LANES] = xp * d_ref[:, p * LANES:(p + 1) * LANES]
        xb = xp.astype(BF16)
        xm_s[:, p, 0:CHUNK, :] = xb * mask3_a
        xm_s[:, p, CHUNK:2 * CHUNK, :] = xb * mask3_b
    bm_s[...] = conv_out_s[2, 0:t, :].astype(BF16).reshape(nc, CHUNK, n)
    cm_s[...] = conv_out_s[3, 0:t, :].astype(BF16).reshape(nc, CHUNK, n)

    ri = lax.broadcasted_iota(jnp.int32, (n, n), 0)
    ci = lax.broadcasted_iota(jnp.int32, (n, n), 1)
    eye = (ri == ci).astype(F32).astype(BF16)
    for c in range(nc):
        bmt_s[c] = lax.dot_general(eye, bm_s[c], NT_DIMS, preferred_element_type=F32).astype(BF16)

    lower = ci <= ri
    upper = ci >= ri
    lo = lax.broadcasted_iota(jnp.int32, (CHUNK, LANES), 1) < SSD_HEAD_DIM

    for c in range(nc):
        g_s[c] = lax.dot_general(cm_s[c], bm_s[c], NT_DIMS, preferred_element_type=F32)

    def chunk_step(c, direction, h, g):
        mask = lower if direction == 0 else upper
        off = direction * HEADS_PER_GROUP
        bt = bmt_s[c]
        ch = jnp.dot(cm_s[c], h.astype(BF16), preferred_element_type=F32)
        rows = row_ref[0, 0, c]
        edge = CHUNK - 1 if direction == 0 else 0
        dstate, decay_h = [], []
        for p in range(n_pairs):
            m_parts, a_cols, b_parts = [], [], []
            for hh in (2 * p, 2 * p + 1):
                k = off + hh
                a_col = jnp.broadcast_to(rows[ROW_A2 + k:ROW_A2 + k + 1, :], (CHUNK, CHUNK)).T
                a_row = rows[ROW_A + k:ROW_A + k + 1, :]
                w_row = rows[ROW_W + k:ROW_W + k + 1, :]
                decay = jnp.exp2(jnp.where(mask, a_col - a_row, -jnp.inf))
                m_parts.append((g * decay).astype(BF16))
                a_cols.append(a_col)
                b_parts.append(bt * w_row.astype(BF16))
            xm = xm_s[c, p]
            y_in = jnp.dot(jnp.concatenate(m_parts, axis=1), xm, preferred_element_type=F32)
            e_pair = jnp.exp2(jnp.where(lo, a_cols[0], a_cols[1]))
            y_off = ch[:, p * LANES:(p + 1) * LANES] * e_pair
            y_s[c, :, p * LANES:(p + 1) * LANES] += y_in + y_off
            dstate.append(jnp.dot(jnp.concatenate(b_parts, axis=1), xm, preferred_element_type=F32))
            decay_h.append(e_pair[edge:edge + 1, :])
        return h * jnp.concatenate(decay_h, axis=1) + jnp.concatenate(dstate, axis=1)

    def body(it, carry):
        h_f, h_b = carry
        return chunk_step(it, 0, h_f, g_s[it]), chunk_step(nc - 1 - it, 1, h_b, g_s[nc - 1 - it])

    h0 = jnp.zeros((n, GROUP_W), F32)
    lax.fori_loop(0, nc, body, (h0, h0), unroll=True)

    y = y_s[...].reshape(t, GROUP_W) * z_ref[0].astype(F32)
    o_ref[0] = _rms(y, nrm_ref[...]).astype(BF16)


def _ssd(proj, conv_w, conv_b, rowpack, d_rep, ssd_norm, b, t):
    nc = t // CHUNK
    xs_tile0 = T_XS * PROJ_TILE // GROUP_W
    z_tile0 = T_Z * PROJ_TILE // GROUP_W
    bm_tile0 = T_BM * PROJ_TILE // SSD_STATE
    cm_tile0 = T_CM * PROJ_TILE // SSD_STATE
    cb0 = D_INNER // SSD_STATE
    cc0 = cb0 + SSD_GROUPS
    seg = t // 8 + (4 - t // 8) % 8
    return pl.pallas_call(
        _ssd_kernel,
        grid=(b, SSD_GROUPS),
        in_specs=[
            pl.BlockSpec((1, t, GROUP_W), lambda i, g: (i, 0, xs_tile0 + g)),
            pl.BlockSpec((1, t, SSD_STATE), lambda i, g: (i, 0, bm_tile0 + g)),
            pl.BlockSpec((1, t, SSD_STATE), lambda i, g: (i, 0, cm_tile0 + g)),
            pl.BlockSpec((1, t, GROUP_W), lambda i, g: (i, 0, z_tile0 + g)),
            pl.BlockSpec((SSD_CONV, GROUP_W), lambda i, g: (0, g)),
            pl.BlockSpec((1, GROUP_W), lambda i, g: (0, g)),
            pl.BlockSpec((SSD_CONV, SSD_STATE), lambda i, g: (0, cb0 + g)),
            pl.BlockSpec((1, SSD_STATE), lambda i, g: (0, cb0 + g)),
            pl.BlockSpec((SSD_CONV, SSD_STATE), lambda i, g: (0, cc0 + g)),
            pl.BlockSpec((1, SSD_STATE), lambda i, g: (0, cc0 + g)),
            pl.BlockSpec((1, 1, nc, ROW_PACK, CHUNK), lambda i, g: (i, g, 0, 0, 0)),
            pl.BlockSpec((1, GROUP_W), lambda i, g: (0, g)),
            pl.BlockSpec((1, GROUP_W), lambda i, g: (0, g)),
        ],
        out_specs=pl.BlockSpec((1, t, GROUP_W), lambda i, g: (i, 0, g)),
        out_shape=jax.ShapeDtypeStruct((b, t, D_INNER), BF16),
        scratch_shapes=[
            pltpu.VMEM((4, 8 * (seg + 2), LANES), F32),
            pltpu.VMEM((4, 8 * seg, LANES), F32),
            pltpu.VMEM((nc, HEADS_PER_GROUP // 2, 2 * CHUNK, LANES), BF16),
            pltpu.VMEM((nc, CHUNK, SSD_STATE), BF16),
            pltpu.VMEM((nc, CHUNK, SSD_STATE), BF16),
            pltpu.VMEM((nc, SSD_STATE, CHUNK), BF16),
            pltpu.VMEM((nc, CHUNK, CHUNK), F32),
            pltpu.VMEM((nc, CHUNK, GROUP_W), F32),
        ],
        compiler_params=_params(("parallel", "arbitrary")),
        name="ssd",
    )(proj, proj, proj, proj, conv_w, conv_b, conv_w, conv_b, conv_w, conv_b,
      rowpack, d_rep, ssd_norm)


def _mem_kernel(q_ref, mem_ref, gm_ref, wkv_ref, gk_ref, o_ref, k_s, v_s):
    width = MEM_HEADS * MEM_HEAD_DIM

    @pl.when(pl.program_id(1) == 0)
    def _():
        mn = _rms(mem_ref[0], gm_ref[...]).astype(BF16)
        kv = jnp.dot(mn, wkv_ref[...], preferred_element_type=F32)
        for h in range(MEM_HEADS):
            hs = slice(h * MEM_HEAD_DIM, (h + 1) * MEM_HEAD_DIM)
            k_s[:, hs] = _rms(kv[:, hs], gk_ref[...]).astype(BF16)
        v_s[...] = kv[:, width:].astype(BF16)

    for h in range(MEM_HEADS):
        hs = slice(h * MEM_HEAD_DIM, (h + 1) * MEM_HEAD_DIM)
        s = lax.dot_general(q_ref[0, :, hs], k_s[:, hs], NT_DIMS, preferred_element_type=F32)
        m = jnp.max(s, axis=1, keepdims=True)
        e = jnp.exp2(s - m)
        l = jnp.sum(e, axis=1, keepdims=True)
        o = jnp.dot(e.astype(BF16), v_s[:, hs], preferred_element_type=F32) * (1.0 / l)
        o_ref[0, :, hs] = o.astype(BF16)


def _mem(proj, mem, g_mem, w_kv, g_k, b, t, tq=512):
    width = MEM_HEADS * MEM_HEAD_DIM
    return pl.pallas_call(
        _mem_kernel,
        grid=(b, t // tq),
        in_specs=[
            pl.BlockSpec((1, tq, PROJ_TILE), lambda i, j: (i, j, T_QM)),
            pl.BlockSpec((1, MEM_TOKENS, D_MODEL), lambda i, j: (i, 0, 0)),
            _const_spec((1, D_MODEL)),
            _const_spec((D_MODEL, 2 * width)),
            _const_spec((1, MEM_HEAD_DIM)),
        ],
        out_specs=pl.BlockSpec((1, tq, width), lambda i, j: (i, j, 0)),
        out_shape=jax.ShapeDtypeStruct((b, t, width), BF16),
        scratch_shapes=[pltpu.VMEM((MEM_TOKENS, width), BF16), pltpu.VMEM((MEM_TOKENS, width), BF16)],
        compiler_params=_params(("parallel", "arbitrary")),
        name="mem_attn",
    )(proj, mem, g_mem, w_kv, g_k)


def _merge_kernel(x_ref, ona_ref, ossd_ref, omem_ref, gna_ref, gssd_ref, gmem_ref,
                  wna_ref, wssd_ref, wmem_ref, wout_ref, o_ref):
    merged = gna_ref[...].astype(F32) * jnp.dot(ona_ref[...], wna_ref[...], preferred_element_type=F32)
    merged += gssd_ref[...].astype(F32) * jnp.dot(ossd_ref[...], wssd_ref[...], preferred_element_type=F32)
    merged += gmem_ref[...].astype(F32) * jnp.dot(omem_ref[...], wmem_ref[...], preferred_element_type=F32)
    o_ref[...] = x_ref[...] + jnp.dot(merged.astype(BF16), wout_ref[...], preferred_element_type=F32)


def _merge(x1, o_na, o_ssd, o_mem, proj, w_na, w_ssd, w_mem, w_out, tm=512):
    n = x1.shape[0]
    tok = lambda w: pl.BlockSpec((tm, w), lambda i: (i, 0))
    gate = lambda k: pl.BlockSpec((tm, PROJ_TILE), lambda i: (i, T_GATE + k))
    return pl.pallas_call(
        _merge_kernel,
        grid=(n // tm,),
        in_specs=[tok(D_MODEL), tok(D_MODEL), tok(D_INNER), tok(D_MODEL), gate(0), gate(1), gate(2),
                  _const_spec((D_MODEL, D_MODEL)), _const_spec((D_INNER, D_MODEL)),
                  _const_spec((D_MODEL, D_MODEL)), _const_spec((D_MODEL, D_MODEL))],
        out_specs=tok(D_MODEL),
        out_shape=jax.ShapeDtypeStruct((n, D_MODEL), F32),
        compiler_params=_params(("parallel",)),
        name="merge_out",
    )(x1, o_na, o_ssd, o_mem, proj, proj, proj, w_na, w_ssd, w_mem, w_out)


def _prepare_weights(ffn1_norm, ffn1_w_gate, ffn1_w_up, ffn1_w_down, mix_norm, w_in, na_q_norm, na_k_norm,
                     na_rpb, conv_w, conv_b, dt_bias_f, dt_bias_b, a_log_f, a_log_b, ssd_d, ssd_norm,
                     mem_norm, w_mem_kv, mem_q_norm, mem_k_norm, w_br_na, w_br_ssd, w_br_mem, w_out,
                     ffn2_norm, ffn2_w_gate, ffn2_w_up, ffn2_w_down):
    bf = lambda w: w.astype(BF16)
    row = lambda v: v.astype(F32).reshape(1, -1)
    dt0 = 3 * NA_HEADS * NA_HEAD_DIM + MEM_HEADS * MEM_HEAD_DIM + D_INNER + D_INNER + 2 * SSD_GROUPS * SSD_STATE
    n_dt = 2 * SSD_HEADS
    w_main = bf(jnp.concatenate([w_in[:, :dt0], w_in[:, dt0 + n_dt:]], axis=1))
    perm = np.concatenate([np.concatenate([np.arange(g * HEADS_PER_GROUP, (g + 1) * HEADS_PER_GROUP),
                                           SSD_HEADS + np.arange(g * HEADS_PER_GROUP, (g + 1) * HEADS_PER_GROUP)])
                           for g in range(SSD_GROUPS)])
    w_dt = w_in[:, dt0:dt0 + n_dt][:, perm]
    wdt_r = bf(jnp.pad(w_dt, ((0, 0), (0, LANES - n_dt)))).T
    pad_v = lambda v: jnp.pad(v.astype(F32)[perm], (0, LANES - n_dt)).reshape(1, LANES)
    dt_bias = pad_v(jnp.concatenate([dt_bias_f, dt_bias_b]))
    a_neg = pad_v(jnp.concatenate([-jnp.exp(a_log_f.astype(F32)), -jnp.exp(a_log_b.astype(F32))]))
    is_fwd = pad_v(jnp.concatenate([jnp.ones((SSD_HEADS,), F32), jnp.zeros((SSD_HEADS,), F32)]))
    gains = jnp.stack([
        jnp.tile(na_q_norm.astype(F32), NA_HEADS) * (NA_HEAD_DIM ** -0.5 * LOG2E),
        jnp.tile(na_k_norm.astype(F32), NA_HEADS),
        jnp.ones((PROJ_TILE,), F32),
        jnp.tile(mem_q_norm.astype(F32), MEM_HEADS) * (MEM_HEAD_DIM ** -0.5 * LOG2E),
    ]).reshape(4, 1, PROJ_TILE)
    blk = np.arange(256) // NA_HEAD_DIM
    nmat64 = (blk[:, None] == blk[None, :]).astype(np.float32) / NA_HEAD_DIM
    nmat256 = np.full((256, 256), 1.0 / MEM_HEAD_DIM, np.float32)
    nmats = jnp.asarray(np.stack([nmat64, nmat256]), BF16)
    return dict(
        ffn1=(row(ffn1_norm), bf(ffn1_w_gate), bf(ffn1_w_up), bf(ffn1_w_down), row(mix_norm)),
        ffn2=(row(ffn2_norm), bf(ffn2_w_gate), bf(ffn2_w_up), bf(ffn2_w_down), row(ffn2_norm)),
        inproj=(w_main, gains, nmats, wdt_r),
        dt=(dt_bias, a_neg, is_fwd),
        na_bias=_na_bias_table(na_rpb),
        conv=(conv_w.astype(F32), row(conv_b)),
        d_rep=jnp.repeat(ssd_d.astype(F32), SSD_HEAD_DIM).reshape(1, D_INNER),
        ssd_norm=row(ssd_norm),
        mem=(row(mem_norm), bf(w_mem_kv), row(mem_k_norm)),
        merge=(bf(w_br_na), bf(w_br_ssd), bf(w_br_mem), bf(w_out)),
    )


def _encoder_layer(x, mem, w):
    b, t, _ = x.shape
    n = b * t
    x1, u = _ffn(x.reshape(n, D_MODEL), *w["ffn1"], with_u=True)
    proj, dtr = _inproj(u, *w["inproj"])
    rowpack = _dtprep(dtr, *w["dt"], b, t)
    proj3 = proj.reshape(b, t, N_PROJ_TILES * PROJ_TILE)
    o_na = _na(proj3, w["na_bias"], b, t)
    o_ssd = _ssd(proj3, *w["conv"], rowpack, w["d_rep"], w["ssd_norm"], b, t)
    o_mem = _mem(proj3, mem, *w["mem"], b, t)
    x2 = _merge(x1, o_na.reshape(n, -1), o_ssd.reshape(n, -1), o_mem.reshape(n, -1), proj, *w["merge"])
    (y,) = _ffn(x2, *w["ffn2"], with_u=False)
    return y.reshape(b, t, D_MODEL)


def kernel(x_prompt, x_sample, mem_prompt, mem_sample, ffn1_norm, ffn1_w_gate, ffn1_w_up, ffn1_w_down, mix_norm, w_in, na_q_norm, na_k_norm, na_rpb, conv_w, conv_b, dt_bias_f, dt_bias_b, a_log_f, a_log_b, ssd_d, ssd_norm, mem_norm, w_mem_kv, mem_q_norm, mem_k_norm, w_br_na, w_br_ssd, w_br_mem, w_out, ffn2_norm, ffn2_w_gate, ffn2_w_up, ffn2_w_down):
    layer = (ffn1_norm, ffn1_w_gate, ffn1_w_up, ffn1_w_down, mix_norm, w_in, na_q_norm, na_k_norm, na_rpb,
             conv_w, conv_b, dt_bias_f, dt_bias_b, a_log_f, a_log_b, ssd_d, ssd_norm, mem_norm, w_mem_kv,
             mem_q_norm, mem_k_norm, w_br_na, w_br_ssd, w_br_mem, w_out, ffn2_norm, ffn2_w_gate, ffn2_w_up,
             ffn2_w_down)
    assert all(p.shape[0] == 1 for p in layer), "single-layer model"
    w = _prepare_weights(*[p[0] for p in layer])
    return (_encoder_layer(x_prompt, mem_prompt, w), _encoder_layer(x_sample, mem_sample, w))
```

```python
import functools

import jax
import jax.numpy as jnp
import numpy as np
from jax import lax
from jax.experimental import pallas as pl
from jax.experimental.pallas import tpu as pltpu

F32 = jnp.float32
BF16 = jnp.bfloat16

D_MODEL = 1024
GRID_W = 64
NA_HEADS = 16
NA_HEAD_DIM = 64
WIN_R = 8
WIN_C = 16
D_INNER = 2048
SSD_HEADS = 32
SSD_HEAD_DIM = 64
SSD_GROUPS = 8
SSD_STATE = 128
SSD_CONV = 5
CHUNK = 128
MEM_TOKENS = 256
MEM_HEADS = 4
MEM_HEAD_DIM = 256
D_FF = 2816
RMS_EPS = 1e-6
LOG2E = 1.4426950408889634
NEG_INF = -1e30

LANES = 128
HEADS_PER_GROUP = SSD_HEADS // SSD_GROUPS
GROUP_W = HEADS_PER_GROUP * SSD_HEAD_DIM
PROJ_TILE = 1024
T_Q, T_K, T_V, T_QM, T_Z, T_XS, T_BM, T_CM, T_GATE = 0, 1, 2, 3, 4, 6, 8, 9, 10
N_PROJ_TILES = 13
VMEM_LIMIT = 56 * 1024 * 1024

NT_DIMS = (((1,), (1,)), ((), ()))


def _params(sem, vmem=VMEM_LIMIT):
    return pltpu.CompilerParams(dimension_semantics=sem, vmem_limit_bytes=vmem)


def _const_spec(shape):
    nd = len(shape)
    return pl.BlockSpec(shape, lambda *_: (0,) * nd, pipeline_mode=pl.Buffered(1))


def _sigmoid(x):
    return 1.0 / (1.0 + jnp.exp(-x))


def _sigmoid_tanh(x):
    return 0.5 + 0.5 * jnp.tanh(0.5 * x)


def _silu(x):
    return x * _sigmoid(x)


def _rms(x, g):
    return x * lax.rsqrt(jnp.mean(x * x, axis=-1, keepdims=True) + RMS_EPS) * g


def _ffn_kernel(x_ref, g_ref, wg_ref, wu_ref, wd_ref, g2_ref, o_ref, *u_ref):
    tm = x_ref.shape[0]
    halves = [slice(0, tm // 2), slice(tm // 2, tm)]
    xs = [x_ref[hs, :] for hs in halves]
    xn = [_rms(x, g_ref[...]).astype(BF16) for x in xs]
    gate = [jnp.dot(a, wg_ref[...], preferred_element_type=F32) for a in xn]
    up = [jnp.dot(a, wu_ref[...], preferred_element_type=F32) for a in xn]
    h = [(_silu(g_) * u_).astype(BF16) for g_, u_ in zip(gate, up)]
    for hs, x, hh in zip(halves, xs, h):
        y = x + 0.5 * jnp.dot(hh, wd_ref[...], preferred_element_type=F32)
        o_ref[hs, :] = y
        if u_ref:
            u_ref[0][hs, :] = _rms(y, g2_ref[...]).astype(BF16)


def _ffn(x, g, wg, wu, wd, g2, with_u, tm=512):
    n = x.shape[0]
    tok = pl.BlockSpec((tm, D_MODEL), lambda i: (i, 0))
    out_shape = [jax.ShapeDtypeStruct((n, D_MODEL), F32)]
    out_specs = [tok]
    if with_u:
        out_shape.append(jax.ShapeDtypeStruct((n, D_MODEL), BF16))
        out_specs.append(tok)
    res = pl.pallas_call(
        _ffn_kernel,
        grid=(n // tm,),
        in_specs=[tok, _const_spec((1, D_MODEL)), _const_spec((D_MODEL, D_FF)),
                  _const_spec((D_MODEL, D_FF)), _const_spec((D_FF, D_MODEL)), _const_spec((1, D_MODEL))],
        out_specs=out_specs,
        out_shape=out_shape,
        compiler_params=_params(("parallel",)),
        name="ffn_u" if with_u else "ffn",
    )(x, g, wg, wu, wd, g2)
    return res


NORM_W = 256


def _inproj_kernel(u_ref, w_ref, gain_ref, nmat_ref, wdr_ref, o_ref, dtr_ref):
    j = pl.program_id(1)
    is_gate = j >= T_GATE
    is_silu = (j == T_Z) | (j == T_Z + 1)
    is_norm = (j == T_Q) | (j == T_K) | (j == T_QM)

    @pl.when(j == 0)
    def _():
        dtr_ref[...] = lax.dot_general(wdr_ref[...], u_ref[...], NT_DIMS, preferred_element_type=F32)

    def tile(epilogue):
        u = u_ref[...]
        for c in range(PROJ_TILE // NORM_W):
            cs = slice(c * NORM_W, (c + 1) * NORM_W)
            acc = jnp.dot(u, w_ref[:, cs], preferred_element_type=F32)
            o_ref[:, cs] = epilogue(acc, cs).astype(BF16)

    @pl.when(is_norm)
    def _():
        acc = jnp.dot(u_ref[...], w_ref[...], preferred_element_type=F32)
        for c in range(PROJ_TILE // NORM_W):
            cs = slice(c * NORM_W, (c + 1) * NORM_W)
            a = acc[:, cs]
            ms = jnp.dot((a * a).astype(BF16), nmat_ref[0], preferred_element_type=F32)
            o_ref[:, cs] = (a * lax.rsqrt(ms + RMS_EPS) * gain_ref[0, :, cs]).astype(BF16)

    pl.when(is_silu)(lambda: tile(lambda a, cs: a * _sigmoid_tanh(a)))
    pl.when(is_gate)(lambda: tile(lambda a, cs: _sigmoid_tanh(a)))
    pl.when(jnp.logical_not(is_norm | is_silu | is_gate))(lambda: tile(lambda a, cs: a))


def _inproj(u, w_main, gains, nmats, wdt_r, tm=2048):
    n = u.shape[0]
    return pl.pallas_call(
        _inproj_kernel,
        grid=(n // tm, N_PROJ_TILES),
        in_specs=[
            pl.BlockSpec((tm, D_MODEL), lambda i, j: (i, 0)),
            pl.BlockSpec((D_MODEL, PROJ_TILE), lambda i, j: (0, j)),
            pl.BlockSpec((1, 1, PROJ_TILE), lambda i, j: (jnp.minimum(j, T_QM), 0, 0)),
            pl.BlockSpec((1, 256, 256), lambda i, j: (jnp.where(j == T_QM, 1, 0), 0, 0)),
            pl.BlockSpec((LANES, D_MODEL), lambda i, j: (0, 0)),
        ],
        out_specs=[
            pl.BlockSpec((tm, PROJ_TILE), lambda i, j: (i, j)),
            pl.BlockSpec((LANES, tm), lambda i, j: (0, i)),
        ],
        out_shape=[
            jax.ShapeDtypeStruct((n, N_PROJ_TILES * PROJ_TILE), BF16),
            jax.ShapeDtypeStruct((LANES, n), F32),
        ],
        compiler_params=_params(("parallel", "arbitrary")),
        name="in_proj",
    )(u, w_main, gains, nmats, wdt_r)


def _softplus(x):
    return jnp.maximum(x, 0.0) + jnp.log1p(jnp.exp(-jnp.abs(x)))


def _split3(x):
    p1 = x.astype(BF16)
    r1 = x - p1.astype(F32)
    p2 = r1.astype(BF16)
    p3 = (r1 - p2.astype(F32)).astype(BF16)
    return p1, p2, p3


ROW_A, ROW_W, ROW_A2 = 0, 8, 16
ROW_PACK = 24


def _dtprep_kernel(dtr_ref, br_ref, ar_ref, fr_ref, pack_ref):
    t = dtr_ref.shape[1]
    per_g = 2 * HEADS_PER_GROUP
    li = lax.broadcasted_iota(jnp.int32, (CHUNK, CHUNK), 0)
    ui = lax.broadcasted_iota(jnp.int32, (CHUNK, CHUNK), 1)
    tri_gt = (ui > li).astype(F32).astype(BF16)
    tri_ge = (ui >= li).astype(F32).astype(BF16)
    isf_r = fr_ref[...] > 0.5
    for c in range(t // CHUNK):
        sl = slice(c * CHUNK, (c + 1) * CHUNK)
        dtr = _softplus(dtr_ref[:, sl] + br_ref[...])
        dar = dtr * ar_ref[...]
        pre_r = jnp.zeros((LANES, CHUNK), F32)
        suf_r = jnp.zeros((LANES, CHUNK), F32)
        for p in _split3(dar):
            pre_r += jnp.dot(p, tri_ge, preferred_element_type=F32)
            suf_r += lax.dot_general(p, tri_gt, NT_DIMS, preferred_element_type=F32)
        a2 = jnp.where(isf_r, pre_r, suf_r + dar) * LOG2E
        a_row = a2 - jnp.log2(dtr)
        w_row = jnp.exp(jnp.where(isf_r, suf_r, pre_r - dar)) * dtr
        for g in range(SSD_GROUPS):
            gs = slice(g * per_g, (g + 1) * per_g)
            pack_ref[0, g, c, ROW_A:ROW_A + per_g, :] = a_row[gs]
            pack_ref[0, g, c, ROW_W:ROW_W + per_g, :] = w_row[gs]
            pack_ref[0, g, c, ROW_A2:ROW_A2 + per_g, :] = a2[gs]


def _dtprep(dtr, bias_c, a_c, isf_c, b, t):
    nc = t // CHUNK
    vr = pl.BlockSpec((LANES, 1), lambda i: (0, 0))
    return pl.pallas_call(
        _dtprep_kernel,
        grid=(b,),
        in_specs=[pl.BlockSpec((LANES, t), lambda i: (0, i)), vr, vr, vr],
        out_specs=pl.BlockSpec((1, SSD_GROUPS, nc, ROW_PACK, CHUNK), lambda i: (i, 0, 0, 0, 0)),
        out_shape=jax.ShapeDtypeStruct((b, SSD_GROUPS, nc, ROW_PACK, CHUNK), F32),
        compiler_params=_params(("parallel",)),
        name="dt_prep",
    )(dtr, bias_c.reshape(LANES, 1), a_c.reshape(LANES, 1), isf_c.reshape(LANES, 1))


NA_ROWS_PER_STEP = 16
NA_KEYS = WIN_R * GRID_W
NA_PAIRS_AHEAD = 3


def _na_kernel(q_ref, k_ref, v_ref, bias_ref, o_ref):
    rows = k_ref.shape[1] // GRID_W
    rb = pl.program_id(1)
    lane = lax.broadcasted_iota(jnp.int32, (GRID_W, LANES), 1)
    lo = lane < NA_HEAD_DIM
    lane1 = lax.broadcasted_iota(jnp.int32, (1, LANES), 1)
    mask_a = (lane1 < NA_HEAD_DIM).astype(F32).astype(BF16)
    mask_b = (lane1 >= NA_HEAD_DIM).astype(F32).astype(BF16)

    def row_body(rl, carry):
        r = rb * NA_ROWS_PER_STEP + rl
        rs = jnp.clip(r - WIN_R // 2, 0, rows - WIN_R)
        s0 = (WIN_R - 1) - (r - rs)
        tbl = (s0 % 2) * 8 + s0 // 2
        q_off = pl.multiple_of(rl * GRID_W, GRID_W)
        k_off = pl.multiple_of(rs * GRID_W, GRID_W)
        def scores(p):
            cs = slice(p * LANES, (p + 1) * LANES)
            q2 = q_ref[0, pl.ds(q_off, GRID_W), cs]
            qblk = jnp.concatenate([q2 * mask_a, q2 * mask_b], axis=0)
            k2 = k_ref[0, pl.ds(k_off, NA_KEYS), cs]
            s = lax.dot_general(qblk, k2, NT_DIMS, preferred_element_type=F32)
            bias = jnp.concatenate([bias_ref[p, tbl + i] for i in range(NA_KEYS // LANES)], axis=1)
            return s + bias

        def attend(p, s):
            cs = slice(p * LANES, (p + 1) * LANES)
            v2 = v_ref[0, pl.ds(k_off, NA_KEYS), cs]
            m = jnp.max(s, axis=1, keepdims=True)
            e = jnp.exp2(s - m)
            l = jnp.sum(e, axis=1, keepdims=True)
            o2 = jnp.dot(e.astype(BF16), v2, preferred_element_type=F32) * (1.0 / l)
            o = jnp.where(lo, o2[:GRID_W], o2[GRID_W:])
            o_ref[0, pl.ds(q_off, GRID_W), cs] = o.astype(BF16)

        n_pairs = NA_HEADS // 2
        pending = [scores(p) for p in range(NA_PAIRS_AHEAD)]
        for p in range(n_pairs):
            if p + NA_PAIRS_AHEAD < n_pairs:
                pending.append(scores(p + NA_PAIRS_AHEAD))
            attend(p, pending.pop(0))
        return carry

    lax.fori_loop(0, NA_ROWS_PER_STEP, row_body, 0, unroll=2)


def _na(proj, bias_tbl, b, t):
    blk = NA_ROWS_PER_STEP * GRID_W
    return pl.pallas_call(
        _na_kernel,
        grid=(b, t // blk),
        in_specs=[
            pl.BlockSpec((1, blk, PROJ_TILE), lambda i, r: (i, r, T_Q)),
            pl.BlockSpec((1, t, PROJ_TILE), lambda i, r: (i, 0, T_K)),
            pl.BlockSpec((1, t, PROJ_TILE), lambda i, r: (i, 0, T_V)),
            _const_spec(bias_tbl.shape),
        ],
        out_specs=pl.BlockSpec((1, blk, PROJ_TILE), lambda i, r: (i, r, 0)),
        out_shape=jax.ShapeDtypeStruct((b, t, NA_HEADS * NA_HEAD_DIM), BF16),
        compiler_params=_params(("parallel", "arbitrary")),
        name="na_attn",
    )(proj, proj, proj, bias_tbl)


def _na_bias_table(rpb):
    qc = np.arange(GRID_W)[:, None]
    kc = np.arange(GRID_W)[None, :]
    cs = np.clip(qc - WIN_C // 2, 0, GRID_W - WIN_C)
    valid = (kc >= cs) & (kc < cs + WIN_C)
    side = GRID_W - WIN_C
    rp = jnp.pad(rpb.astype(F32) * LOG2E, ((0, 0), (0, 3), (side, side)))
    t1 = jnp.stack([rp[:, :, GRID_W - 1 - q:2 * GRID_W - 1 - q] for q in range(GRID_W)], axis=2)
    row_ok = (np.arange(2 * WIN_R + 2) < 2 * WIN_R - 1)[None, :, None, None]
    t1 = jnp.where(jnp.asarray(valid[None, None] & row_ok), t1, NEG_INF)
    t2 = jnp.stack([jnp.stack([t1[:, par + jj:par + jj + 16:2] for jj in (0, 1)], axis=3) for par in (0, 1)],
                   axis=1)
    t2 = t2.reshape(NA_HEADS // 2, 2, 2, 8, GRID_W, 2, GRID_W)
    t2 = jnp.transpose(t2, (0, 2, 3, 1, 4, 5, 6))
    return t2.reshape(NA_HEADS // 2, 16, 2 * GRID_W, 2 * GRID_W)


def _ssd_kernel(xs_ref, bm_ref, cm_ref, z_ref, cwx_ref, cbx_ref, cwb_ref, cbb_ref, cwc_ref, cbc_ref,
                row_ref, d_ref, nrm_ref, o_ref,
                conv_in_s, conv_out_s, xm_s, bm_s, cm_s, bmt_s, g_s, y_s):
    t = xs_ref.shape[1]
    nc = t // CHUNK
    n = SSD_STATE
    half = SSD_CONV // 2
    n_pairs = HEADS_PER_GROUP // 2

    seg = conv_in_s.shape[1] // 8 - 2
    slabs = [(xs_ref, 0, cwx_ref, cbx_ref), (xs_ref, LANES, cwx_ref, cbx_ref),
             (bm_ref, 0, cwb_ref, cbb_ref), (cm_ref, 0, cwc_ref, cbc_ref)]
    taps, biases = [], []
    for s, (src_ref, l0, w_ref, b_ref) in enumerate(slabs):
        conv_in_s[s, 0:8, :] = jnp.zeros((8, LANES), F32)
        conv_in_s[s, 8 + t:, :] = jnp.zeros((conv_in_s.shape[1] - 8 - t, LANES), F32)
        conv_in_s[s, 8:8 + t, :] = src_ref[0, :, l0:l0 + LANES].astype(F32)
        taps.append([jnp.broadcast_to(0.5 * w_ref[k:k + 1, l0:l0 + LANES], (8, LANES)) for k in range(SSD_CONV)])
        biases.append(jnp.broadcast_to(0.5 * b_ref[:, l0:l0 + LANES], (8, LANES)))

    def conv_body(j, carry):
        for s in range(len(slabs)):
            acc = biases[s]
            for k in range(SSD_CONV):
                acc = acc + taps[s][k] * conv_in_s[s, pl.ds(8 - half + k + j, 8, stride=seg), :]
            conv_out_s[s, pl.ds(j, 8, stride=seg), :] = acc + acc * jnp.tanh(acc)
        return carry

    lax.fori_loop(0, seg, conv_body, 0, unroll=10)

    lane3 = lax.broadcasted_iota(jnp.int32, (1, 1, LANES), 2)
    mask3_a = (lane3 < SSD_HEAD_DIM).astype(F32).astype(BF16)
    mask3_b = (lane3 >= SSD_HEAD_DIM).astype(F32).astype(BF16)
    for p in range(n_pairs):
        xp = conv_out_s[p, 0:t, :].reshape(nc, CHUNK, LANES)
        y_s[:, :, p * LANES:(p + 1) * LANES] = xp * d_ref[:, p * LANES:(p + 1) * LANES]
        xb = xp.astype(BF16)
        xm_s[:, p, 0:CHUNK, :] = xb * mask3_a
        xm_s[:, p, CHUNK:2 * CHUNK, :] = xb * mask3_b
    bm_s[...] = conv_out_s[2, 0:t, :].astype(BF16).reshape(nc, CHUNK, n)
    cm_s[...] = conv_out_s[3, 0:t, :].astype(BF16).reshape(nc, CHUNK, n)

    ri = lax.broadcasted_iota(jnp.int32, (n, n), 0)
    ci = lax.broadcasted_iota(jnp.int32, (n, n), 1)
    eye = (ri == ci).astype(F32).astype(BF16)
    for c in range(nc):
        bmt_s[c] = lax.dot_general(eye, bm_s[c], NT_DIMS, preferred_element_type=F32).astype(BF16)

    lower = ci <= ri
    upper = ci >= ri
    lo = lax.broadcasted_iota(jnp.int32, (CHUNK, LANES), 1) < SSD_HEAD_DIM

    for c in range(nc):
        g_s[c] = lax.dot_general(cm_s[c], bm_s[c], NT_DIMS, preferred_element_type=F32)

    def chunk_step(c, direction, h, g):
        mask = lower if direction == 0 else upper
        off = direction * HEADS_PER_GROUP
        bt = bmt_s[c]
        ch = jnp.dot(cm_s[c], h.astype(BF16), preferred_element_type=F32)
        rows = row_ref[0, 0, c]
        edge = CHUNK - 1 if direction == 0 else 0
        dstate, decay_h = [], []
        for p in range(n_pairs):
            m_parts, a_cols, b_parts = [], [], []
            for hh in (2 * p, 2 * p + 1):
                k = off + hh
                a_col = jnp.broadcast_to(rows[ROW_A2 + k:ROW_A2 + k + 1, :], (CHUNK, CHUNK)).T
                a_row = rows[ROW_A + k:ROW_A + k + 1, :]
                w_row = rows[ROW_W + k:ROW_W + k + 1, :]
                decay = jnp.exp2(jnp.where(mask, a_col - a_row, -jnp.inf))
                m_parts.append((g * decay).astype(BF16))
                a_cols.append(a_col)
                b_parts.append(bt * w_row.astype(BF16))
            xm = xm_s[c, p]
            y_in = jnp.dot(jnp.concatenate(m_parts, axis=1), xm, preferred_element_type=F32)
            e_pair = jnp.exp2(jnp.where(lo, a_cols[0], a_cols[1]))
            y_off = ch[:, p * LANES:(p + 1) * LANES] * e_pair
            y_s[c, :, p * LANES:(p + 1) * LANES] += y_in + y_off
            dstate.append(jnp.dot(jnp.concatenate(b_parts, axis=1), xm, preferred_element_type=F32))
            decay_h.append(e_pair[edge:edge + 1, :])
        return h * jnp.concatenate(decay_h, axis=1) + jnp.concatenate(dstate, axis=1)

    def body(it, carry):
        h_f, h_b = carry
        return chunk_step(it, 0, h_f, g_s[it]), chunk_step(nc - 1 - it, 1, h_b, g_s[nc - 1 - it])

    h0 = jnp.zeros((n, GROUP_W), F32)
    lax.fori_loop(0, nc, body, (h0, h0), unroll=True)

    y = y_s[...].reshape(t, GROUP_W) * z_ref[0].astype(F32)
    o_ref[0] = _rms(y, nrm_ref[...]).astype(BF16)


def _ssd(proj, conv_w, conv_b, rowpack, d_rep, ssd_norm, b, t):
    nc = t // CHUNK
    xs_tile0 = T_XS * PROJ_TILE // GROUP_W
    z_tile0 = T_Z * PROJ_TILE // GROUP_W
    bm_tile0 = T_BM * PROJ_TILE // SSD_STATE
    cm_tile0 = T_CM * PROJ_TILE // SSD_STATE
    cb0 = D_INNER // SSD_STATE
    cc0 = cb0 + SSD_GROUPS
    seg = t // 8 + (4 - t // 8) % 8
    return pl.pallas_call(
        _ssd_kernel,
        grid=(b, SSD_GROUPS),
        in_specs=[
            pl.BlockSpec((1, t, GROUP_W), lambda i, g: (i, 0, xs_tile0 + g)),
            pl.BlockSpec((1, t, SSD_STATE), lambda i, g: (i, 0, bm_tile0 + g)),
            pl.BlockSpec((1, t, SSD_STATE), lambda i, g: (i, 0, cm_tile0 + g)),
            pl.BlockSpec((1, t, GROUP_W), lambda i, g: (i, 0, z_tile0 + g)),
            pl.BlockSpec((SSD_CONV, GROUP_W), lambda i, g: (0, g)),
            pl.BlockSpec((1, GROUP_W), lambda i, g: (0, g)),
            pl.BlockSpec((SSD_CONV, SSD_STATE), lambda i, g: (0, cb0 + g)),
            pl.BlockSpec((1, SSD_STATE), lambda i, g: (0, cb0 + g)),
            pl.BlockSpec((SSD_CONV, SSD_STATE), lambda i, g: (0, cc0 + g)),
            pl.BlockSpec((1, SSD_STATE), lambda i, g: (0, cc0 + g)),
            pl.BlockSpec((1, 1, nc, ROW_PACK, CHUNK), lambda i, g: (i, g, 0, 0, 0)),
            pl.BlockSpec((1, GROUP_W), lambda i, g: (0, g)),
            pl.BlockSpec((1, GROUP_W), lambda i, g: (0, g)),
        ],
        out_specs=pl.BlockSpec((1, t, GROUP_W), lambda i, g: (i, 0, g)),
        out_shape=jax.ShapeDtypeStruct((b, t, D_INNER), BF16),
        scratch_shapes=[
            pltpu.VMEM((4, 8 * (seg + 2), LANES), F32),
            pltpu.VMEM((4, 8 * seg, LANES), F32),
            pltpu.VMEM((nc, HEADS_PER_GROUP // 2, 2 * CHUNK, LANES), BF16),
            pltpu.VMEM((nc, CHUNK, SSD_STATE), BF16),
            pltpu.VMEM((nc, CHUNK, SSD_STATE), BF16),
            pltpu.VMEM((nc, SSD_STATE, CHUNK), BF16),
            pltpu.VMEM((nc, CHUNK, CHUNK), F32),
            pltpu.VMEM((nc, CHUNK, GROUP_W), F32),
        ],
        compiler_params=_params(("parallel", "arbitrary")),
        name="ssd",
    )(proj, proj, proj, proj, conv_w, conv_b, conv_w, conv_b, conv_w, conv_b,
      rowpack, d_rep, ssd_norm)


def _mem_kernel(q_ref, mem_ref, gm_ref, wkv_ref, gk_ref, o_ref, k_s, v_s):
    width = MEM_HEADS * MEM_HEAD_DIM

    @pl.when(pl.program_id(1) == 0)
    def _():
        mn = _rms(mem_ref[0], gm_ref[...]).astype(BF16)
        kv = jnp.dot(mn, wkv_ref[...], preferred_element_type=F32)
        for h in range(MEM_HEADS):
            hs = slice(h * MEM_HEAD_DIM, (h + 1) * MEM_HEAD_DIM)
            k_s[:, hs] = _rms(kv[:, hs], gk_ref[...]).astype(BF16)
        v_s[...] = kv[:, width:].astype(BF16)

    for h in range(MEM_HEADS):
        hs = slice(h * MEM_HEAD_DIM, (h + 1) * MEM_HEAD_DIM)
        s = lax.dot_general(q_ref[0, :, hs], k_s[:, hs], NT_DIMS, preferred_element_type=F32)
        m = jnp.max(s, axis=1, keepdims=True)
        e = jnp.exp2(s - m)
        l = jnp.sum(e, axis=1, keepdims=True)
        o = jnp.dot(e.astype(BF16), v_s[:, hs], preferred_element_type=F32) * (1.0 / l)
        o_ref[0, :, hs] = o.astype(BF16)


def _mem(proj, mem, g_mem, w_kv, g_k, b, t, tq=1024):
    width = MEM_HEADS * MEM_HEAD_DIM
    return pl.pallas_call(
        _mem_kernel,
        grid=(b, t // tq),
        in_specs=[
            pl.BlockSpec((1, tq, PROJ_TILE), lambda i, j: (i, j, T_QM)),
            pl.BlockSpec((1, MEM_TOKENS, D_MODEL), lambda i, j: (i, 0, 0)),
            _const_spec((1, D_MODEL)),
            _const_spec((D_MODEL, 2 * width)),
            _const_spec((1, MEM_HEAD_DIM)),
        ],
        out_specs=pl.BlockSpec((1, tq, width), lambda i, j: (i, j, 0)),
        out_shape=jax.ShapeDtypeStruct((b, t, width), BF16),
        scratch_shapes=[pltpu.VMEM((MEM_TOKENS, width), BF16), pltpu.VMEM((MEM_TOKENS, width), BF16)],
        compiler_params=_params(("parallel", "arbitrary")),
        name="mem_attn",
    )(proj, mem, g_mem, w_kv, g_k)


def _merge_kernel(x_ref, ona_ref, ossd_ref, omem_ref, gna_ref, gssd_ref, gmem_ref,
                  wna_ref, wssd_ref, wmem_ref, wout_ref, o_ref):
    merged = gna_ref[...].astype(F32) * jnp.dot(ona_ref[...], wna_ref[...], preferred_element_type=F32)
    merged += gssd_ref[...].astype(F32) * jnp.dot(ossd_ref[...], wssd_ref[...], preferred_element_type=F32)
    merged += gmem_ref[...].astype(F32) * jnp.dot(omem_ref[...], wmem_ref[...], preferred_element_type=F32)
    o_ref[...] = x_ref[...] + jnp.dot(merged.astype(BF16), wout_ref[...], preferred_element_type=F32)


def _merge(x1, o_na, o_ssd, o_mem, proj, w_na, w_ssd, w_mem, w_out, tm=512):
    n = x1.shape[0]
    tok = lambda w: pl.BlockSpec((tm, w), lambda i: (i, 0))
    gate = lambda k: pl.BlockSpec((tm, PROJ_TILE), lambda i: (i, T_GATE + k))
    return pl.pallas_call(
        _merge_kernel,
        grid=(n // tm,),
        in_specs=[tok(D_MODEL), tok(D_MODEL), tok(D_INNER), tok(D_MODEL), gate(0), gate(1), gate(2),
                  _const_spec((D_MODEL, D_MODEL)), _const_spec((D_INNER, D_MODEL)),
                  _const_spec((D_MODEL, D_MODEL)), _const_spec((D_MODEL, D_MODEL))],
        out_specs=tok(D_MODEL),
        out_shape=jax.ShapeDtypeStruct((n, D_MODEL), F32),
        compiler_params=_params(("parallel",)),
        name="merge_out",
    )(x1, o_na, o_ssd, o_mem, proj, proj, proj, w_na, w_ssd, w_mem, w_out)


def _prepare_weights(ffn1_norm, ffn1_w_gate, ffn1_w_up, ffn1_w_down, mix_norm, w_in, na_q_norm, na_k_norm,
                     na_rpb, conv_w, conv_b, dt_bias_f, dt_bias_b, a_log_f, a_log_b, ssd_d, ssd_norm,
                     mem_norm, w_mem_kv, mem_q_norm, mem_k_norm, w_br_na, w_br_ssd, w_br_mem, w_out,
                     ffn2_norm, ffn2_w_gate, ffn2_w_up, ffn2_w_down):
    bf = lambda w: w.astype(BF16)
    row = lambda v: v.astype(F32).reshape(1, -1)
    dt0 = 3 * NA_HEADS * NA_HEAD_DIM + MEM_HEADS * MEM_HEAD_DIM + D_INNER + D_INNER + 2 * SSD_GROUPS * SSD_STATE
    n_dt = 2 * SSD_HEADS
    w_main = bf(jnp.concatenate([w_in[:, :dt0], w_in[:, dt0 + n_dt:]], axis=1))
    perm = np.concatenate([np.concatenate([np.arange(g * HEADS_PER_GROUP, (g + 1) * HEADS_PER_GROUP),
                                           SSD_HEADS + np.arange(g * HEADS_PER_GROUP, (g + 1) * HEADS_PER_GROUP)])
                           for g in range(SSD_GROUPS)])
    w_dt = w_in[:, dt0:dt0 + n_dt][:, perm]
    wdt_r = bf(jnp.pad(w_dt, ((0, 0), (0, LANES - n_dt)))).T
    pad_v = lambda v: jnp.pad(v.astype(F32)[perm], (0, LANES - n_dt)).reshape(1, LANES)
    dt_bias = pad_v(jnp.concatenate([dt_bias_f, dt_bias_b]))
    a_neg = pad_v(jnp.concatenate([-jnp.exp(a_log_f.astype(F32)), -jnp.exp(a_log_b.astype(F32))]))
    is_fwd = pad_v(jnp.concatenate([jnp.ones((SSD_HEADS,), F32), jnp.zeros((SSD_HEADS,), F32)]))
    gains = jnp.stack([
        jnp.tile(na_q_norm.astype(F32), NA_HEADS) * (NA_HEAD_DIM ** -0.5 * LOG2E),
        jnp.tile(na_k_norm.astype(F32), NA_HEADS),
        jnp.ones((PROJ_TILE,), F32),
        jnp.tile(mem_q_norm.astype(F32), MEM_HEADS) * (MEM_HEAD_DIM ** -0.5 * LOG2E),
    ]).reshape(4, 1, PROJ_TILE)
    blk = np.arange(256) // NA_HEAD_DIM
    nmat64 = (blk[:, None] == blk[None, :]).astype(np.float32) / NA_HEAD_DIM
    nmat256 = np.full((256, 256), 1.0 / MEM_HEAD_DIM, np.float32)
    nmats = jnp.asarray(np.stack([nmat64, nmat256]), BF16)
    return dict(
        ffn1=(row(ffn1_norm), bf(ffn1_w_gate), bf(ffn1_w_up), bf(ffn1_w_down), row(mix_norm)),
        ffn2=(row(ffn2_norm), bf(ffn2_w_gate), bf(ffn2_w_up), bf(ffn2_w_down), row(ffn2_norm)),
        inproj=(w_main, gains, nmats, wdt_r),
        dt=(dt_bias, a_neg, is_fwd),
        na_bias=_na_bias_table(na_rpb),
        conv=(conv_w.astype(F32), row(conv_b)),
        d_rep=jnp.repeat(ssd_d.astype(F32), SSD_HEAD_DIM).reshape(1, D_INNER),
        ssd_norm=row(ssd_norm),
        mem=(row(mem_norm), bf(w_mem_kv), row(mem_k_norm)),
        merge=(bf(w_br_na), bf(w_br_ssd), bf(w_br_mem), bf(w_out)),
    )


def _encoder_layer(x, mem, w):
    b, t, _ = x.shape
    n = b * t
    x1, u = _ffn(x.reshape(n, D_MODEL), *w["ffn1"], with_u=True)
    proj, dtr = _inproj(u, *w["inproj"])
    rowpack = _dtprep(dtr, *w["dt"], b, t)
    proj3 = proj.reshape(b, t, N_PROJ_TILES * PROJ_TILE)
    o_na = _na(proj3, w["na_bias"], b, t)
    o_ssd = _ssd(proj3, *w["conv"], rowpack, w["d_rep"], w["ssd_norm"], b, t)
    o_mem = _mem(proj3, mem, *w["mem"], b, t)
    x2 = _merge(x1, o_na.reshape(n, -1), o_ssd.reshape(n, -1), o_mem.reshape(n, -1), proj, *w["merge"])
    (y,) = _ffn(x2, *w["ffn2"], with_u=False)
    return y.reshape(b, t, D_MODEL)


def kernel(x_prompt, x_sample, mem_prompt, mem_sample, ffn1_norm, ffn1_w_gate, ffn1_w_up, ffn1_w_down, mix_norm, w_in, na_q_norm, na_k_norm, na_rpb, conv_w, conv_b, dt_bias_f, dt_bias_b, a_log_f, a_log_b, ssd_d, ssd_norm, mem_norm, w_mem_kv, mem_q_norm, mem_k_norm, w_br_na, w_br_ssd, w_br_mem, w_out, ffn2_norm, ffn2_w_gate, ffn2_w_up, ffn2_w_down):
    layer = (ffn1_norm, ffn1_w_gate, ffn1_w_up, ffn1_w_down, mix_norm, w_in, na_q_norm, na_k_norm, na_rpb,
             conv_w, conv_b, dt_bias_f, dt_bias_b, a_log_f, a_log_b, ssd_d, ssd_norm, mem_norm, w_mem_kv,
             mem_q_norm, mem_k_norm, w_br_na, w_br_ssd, w_br_mem, w_out, ffn2_norm, ffn2_w_gate, ffn2_w_up,
             ffn2_w_down)
    assert all(p.shape[0] == 1 for p in layer), "single-layer model"
    w = _prepare_weights(*[p[0] for p in layer])
    return (_encoder_layer(x_prompt, mem_prompt, w), _encoder_layer(x_sample, mem_sample, w))
```

```python
import functools

import jax
import jax.numpy as jnp
import numpy as np
from jax import lax
from jax.experimental import pallas as pl
from jax.experimental.pallas import tpu as pltpu

F32 = jnp.float32
BF16 = jnp.bfloat16

D_MODEL = 1024
GRID_W = 64
NA_HEADS = 16
NA_HEAD_DIM = 64
WIN_R = 8
WIN_C = 16
D_INNER = 2048
SSD_HEADS = 32
SSD_HEAD_DIM = 64
SSD_GROUPS = 8
SSD_STATE = 128
SSD_CONV = 5
CHUNK = 128
MEM_TOKENS = 256
MEM_HEADS = 4
MEM_HEAD_DIM = 256
D_FF = 2816
RMS_EPS = 1e-6
LOG2E = 1.4426950408889634
NEG_INF = -1e30

LANES = 128
HEADS_PER_GROUP = SSD_HEADS // SSD_GROUPS
GROUP_W = HEADS_PER_GROUP * SSD_HEAD_DIM
PROJ_TILE = 1024
T_Q, T_K, T_V, T_QM, T_Z, T_XS, T_BM, T_CM, T_GATE = 0, 1, 2, 3, 4, 6, 8, 9, 10
N_PROJ_TILES = 13
VMEM_LIMIT = 56 * 1024 * 1024

NT_DIMS = (((1,), (1,)), ((), ()))


def _params(sem, vmem=VMEM_LIMIT):
    return pltpu.CompilerParams(dimension_semantics=sem, vmem_limit_bytes=vmem)


def _const_spec(shape):
    nd = len(shape)
    return pl.BlockSpec(shape, lambda *_: (0,) * nd, pipeline_mode=pl.Buffered(1))


def _sigmoid(x):
    return 1.0 / (1.0 + jnp.exp(-x))


def _sigmoid_tanh(x):
    return 0.5 + 0.5 * jnp.tanh(0.5 * x)


def _silu(x):
    return x * _sigmoid(x)


def _rms(x, g):
    return x * lax.rsqrt(jnp.mean(x * x, axis=-1, keepdims=True) + RMS_EPS) * g


def _ffn_kernel(x_ref, g_ref, wg_ref, wu_ref, wd_ref, g2_ref, o_ref, *u_ref):
    tm = x_ref.shape[0]
    halves = [slice(0, tm // 2), slice(tm // 2, tm)]
    xs = [x_ref[hs, :] for hs in halves]
    xn = [_rms(x, g_ref[...]).astype(BF16) for x in xs]
    gate = [jnp.dot(a, wg_ref[...], preferred_element_type=F32) for a in xn]
    up = [jnp.dot(a, wu_ref[...], preferred_element_type=F32) for a in xn]
    h = [(_silu(g_) * u_).astype(BF16) for g_, u_ in zip(gate, up)]
    for hs, x, hh in zip(halves, xs, h):
        y = x + 0.5 * jnp.dot(hh, wd_ref[...], preferred_element_type=F32)
        o_ref[hs, :] = y
        if u_ref:
            u_ref[0][hs, :] = _rms(y, g2_ref[...]).astype(BF16)


def _ffn(x, g, wg, wu, wd, g2, with_u, tm=512):
    n = x.shape[0]
    tok = pl.BlockSpec((tm, D_MODEL), lambda i: (i, 0))
    out_shape = [jax.ShapeDtypeStruct((n, D_MODEL), F32)]
    out_specs = [tok]
    if with_u:
        out_shape.append(jax.ShapeDtypeStruct((n, D_MODEL), BF16))
        out_specs.append(tok)
    res = pl.pallas_call(
        _ffn_kernel,
        grid=(n // tm,),
        in_specs=[tok, _const_spec((1, D_MODEL)), _const_spec((D_MODEL, D_FF)),
                  _const_spec((D_MODEL, D_FF)), _const_spec((D_FF, D_MODEL)), _const_spec((1, D_MODEL))],
        out_specs=out_specs,
        out_shape=out_shape,
        compiler_params=_params(("parallel",)),
        name="ffn_u" if with_u else "ffn",
    )(x, g, wg, wu, wd, g2)
    return res


NORM_W = 256


def _inproj_kernel(u_ref, w_ref, gain_ref, nmat_ref, wdr_ref, o_ref, dtr_ref):
    j = pl.program_id(1)
    is_gate = j >= T_GATE
    is_silu = (j == T_Z) | (j == T_Z + 1)
    is_norm = (j == T_Q) | (j == T_K) | (j == T_QM)

    @pl.when(j == 0)
    def _():
        dtr_ref[...] = lax.dot_general(wdr_ref[...], u_ref[...], NT_DIMS, preferred_element_type=F32)

    def tile(epilogue):
        u = u_ref[...]
        for c in range(PROJ_TILE // NORM_W):
            cs = slice(c * NORM_W, (c + 1) * NORM_W)
            acc = jnp.dot(u, w_ref[:, cs], preferred_element_type=F32)
            o_ref[:, cs] = epilogue(acc, cs).astype(BF16)

    @pl.when(is_norm)
    def _():
        acc = jnp.dot(u_ref[...], w_ref[...], preferred_element_type=F32)
        for c in range(PROJ_TILE // NORM_W):
            cs = slice(c * NORM_W, (c + 1) * NORM_W)
            a = acc[:, cs]
            ms = jnp.dot((a * a).astype(BF16), nmat_ref[0], preferred_element_type=F32)
            o_ref[:, cs] = (a * lax.rsqrt(ms + RMS_EPS) * gain_ref[0, :, cs]).astype(BF16)

    pl.when(is_silu)(lambda: tile(lambda a, cs: a * _sigmoid_tanh(a)))
    pl.when(is_gate)(lambda: tile(lambda a, cs: _sigmoid_tanh(a)))
    pl.when(jnp.logical_not(is_norm | is_silu | is_gate))(lambda: tile(lambda a, cs: a))


def _inproj(u, w_main, gains, nmats, wdt_r, tm=2048):
    n = u.shape[0]
    return pl.pallas_call(
        _inproj_kernel,
        grid=(n // tm, N_PROJ_TILES),
        in_specs=[
            pl.BlockSpec((tm, D_MODEL), lambda i, j: (i, 0)),
            pl.BlockSpec((D_MODEL, PROJ_TILE), lambda i, j: (0, j)),
            pl.BlockSpec((1, 1, PROJ_TILE), lambda i, j: (jnp.minimum(j, T_QM), 0, 0)),
            pl.BlockSpec((1, 256, 256), lambda i, j: (jnp.where(j == T_QM, 1, 0), 0, 0)),
            pl.BlockSpec((LANES, D_MODEL), lambda i, j: (0, 0)),
        ],
        out_specs=[
            pl.BlockSpec((tm, PROJ_TILE), lambda i, j: (i, j)),
            pl.BlockSpec((LANES, tm), lambda i, j: (0, i)),
        ],
        out_shape=[
            jax.ShapeDtypeStruct((n, N_PROJ_TILES * PROJ_TILE), BF16),
            jax.ShapeDtypeStruct((LANES, n), F32),
        ],
        compiler_params=_params(("parallel", "arbitrary")),
        name="in_proj",
    )(u, w_main, gains, nmats, wdt_r)


def _softplus(x):
    return jnp.maximum(x, 0.0) + jnp.log1p(jnp.exp(-jnp.abs(x)))


def _split3(x):
    p1 = x.astype(BF16)
    r1 = x - p1.astype(F32)
    p2 = r1.astype(BF16)
    p3 = (r1 - p2.astype(F32)).astype(BF16)
    return p1, p2, p3


ROW_A, ROW_W, ROW_A2 = 0, 8, 16
ROW_PACK = 24


def _dtprep_kernel(dtr_ref, br_ref, ar_ref, fr_ref, pack_ref):
    t = dtr_ref.shape[1]
    per_g = 2 * HEADS_PER_GROUP
    li = lax.broadcasted_iota(jnp.int32, (CHUNK, CHUNK), 0)
    ui = lax.broadcasted_iota(jnp.int32, (CHUNK, CHUNK), 1)
    tri_gt = (ui > li).astype(F32).astype(BF16)
    tri_ge = (ui >= li).astype(F32).astype(BF16)
    isf_r = fr_ref[...] > 0.5
    for c in range(t // CHUNK):
        sl = slice(c * CHUNK, (c + 1) * CHUNK)
        dtr = _softplus(dtr_ref[:, sl] + br_ref[...])
        dar = dtr * ar_ref[...]
        pre_r = jnp.zeros((LANES, CHUNK), F32)
        suf_r = jnp.zeros((LANES, CHUNK), F32)
        for p in _split3(dar):
            pre_r += jnp.dot(p, tri_ge, preferred_element_type=F32)
            suf_r += lax.dot_general(p, tri_gt, NT_DIMS, preferred_element_type=F32)
        a2 = jnp.where(isf_r, pre_r, suf_r + dar) * LOG2E
        a_row = a2 - jnp.log2(dtr)
        w_row = jnp.exp(jnp.where(isf_r, suf_r, pre_r - dar)) * dtr
        for g in range(SSD_GROUPS):
            gs = slice(g * per_g, (g + 1) * per_g)
            pack_ref[0, g, c, ROW_A:ROW_A + per_g, :] = a_row[gs]
            pack_ref[0, g, c, ROW_W:ROW_W + per_g, :] = w_row[gs]
            pack_ref[0, g, c, ROW_A2:ROW_A2 + per_g, :] = a2[gs]


def _dtprep(dtr, bias_c, a_c, isf_c, b, t):
    nc = t // CHUNK
    vr = pl.BlockSpec((LANES, 1), lambda i: (0, 0))
    return pl.pallas_call(
        _dtprep_kernel,
        grid=(b,),
        in_specs=[pl.BlockSpec((LANES, t), lambda i: (0, i)), vr, vr, vr],
        out_specs=pl.BlockSpec((1, SSD_GROUPS, nc, ROW_PACK, CHUNK), lambda i: (i, 0, 0, 0, 0)),
        out_shape=jax.ShapeDtypeStruct((b, SSD_GROUPS, nc, ROW_PACK, CHUNK), F32),
        compiler_params=_params(("parallel",)),
        name="dt_prep",
    )(dtr, bias_c.reshape(LANES, 1), a_c.reshape(LANES, 1), isf_c.reshape(LANES, 1))


NA_ROWS_PER_STEP = 16
NA_KEYS = WIN_R * GRID_W
NA_PAIRS_AHEAD = 3


def _na_kernel(q_ref, k_ref, v_ref, bias_ref, o_ref):
    rows = k_ref.shape[1] // GRID_W
    rb = pl.program_id(1)
    lane = lax.broadcasted_iota(jnp.int32, (GRID_W, LANES), 1)
    lo = lane < NA_HEAD_DIM
    lane1 = lax.broadcasted_iota(jnp.int32, (1, LANES), 1)
    mask_a = (lane1 < NA_HEAD_DIM).astype(F32).astype(BF16)
    mask_b = (lane1 >= NA_HEAD_DIM).astype(F32).astype(BF16)

    def row_body(rl, carry):
        r = rb * NA_ROWS_PER_STEP + rl
        rs = jnp.clip(r - WIN_R // 2, 0, rows - WIN_R)
        s0 = (WIN_R - 1) - (r - rs)
        tbl = (s0 % 2) * 8 + s0 // 2
        q_off = pl.multiple_of(rl * GRID_W, GRID_W)
        k_off = pl.multiple_of(rs * GRID_W, GRID_W)
        def scores(p):
            cs = slice(p * LANES, (p + 1) * LANES)
            q2 = q_ref[0, pl.ds(q_off, GRID_W), cs]
            qblk = jnp.concatenate([q2 * mask_a, q2 * mask_b], axis=0)
            k2 = k_ref[0, pl.ds(k_off, NA_KEYS), cs]
            s = lax.dot_general(qblk, k2, NT_DIMS, preferred_element_type=F32)
            bias = jnp.concatenate([bias_ref[p, tbl + i] for i in range(NA_KEYS // LANES)], axis=1)
            return s + bias

        def attend(p, s):
            cs = slice(p * LANES, (p + 1) * LANES)
            v2 = v_ref[0, pl.ds(k_off, NA_KEYS), cs]
            m = jnp.max(s, axis=1, keepdims=True)
            e = jnp.exp2(s - m)
            l = jnp.sum(e, axis=1, keepdims=True)
            o2 = jnp.dot(e.astype(BF16), v2, preferred_element_type=F32) * (1.0 / l)
            o = jnp.where(lo, o2[:GRID_W], o2[GRID_W:])
            o_ref[0, pl.ds(q_off, GRID_W), cs] = o.astype(BF16)

        n_pairs = NA_HEADS // 2
        pending = [scores(p) for p in range(NA_PAIRS_AHEAD)]
        for p in range(n_pairs):
            if p + NA_PAIRS_AHEAD < n_pairs:
                pending.append(scores(p + NA_PAIRS_AHEAD))
            attend(p, pending.pop(0))
        return carry

    lax.fori_loop(0, NA_ROWS_PER_STEP, row_body, 0, unroll=2)


def _na(proj, bias_tbl, b, t):
    blk = NA_ROWS_PER_STEP * GRID_W
    return pl.pallas_call(
        _na_kernel,
        grid=(b, t // blk),
        in_specs=[
            pl.BlockSpec((1, blk, PROJ_TILE), lambda i, r: (i, r, T_Q)),
            pl.BlockSpec((1, t, PROJ_TILE), lambda i, r: (i, 0, T_K)),
            pl.BlockSpec((1, t, PROJ_TILE), lambda i, r: (i, 0, T_V)),
            _const_spec(bias_tbl.shape),
        ],
        out_specs=pl.BlockSpec((1, blk, PROJ_TILE), lambda i, r: (i, r, 0)),
        out_shape=jax.ShapeDtypeStruct((b, t, NA_HEADS * NA_HEAD_DIM), BF16),
        compiler_params=_params(("parallel", "arbitrary")),
        name="na_attn",
    )(proj, proj, proj, bias_tbl)


def _na_bias_table(rpb):
    qc = np.arange(GRID_W)[:, None]
    kc = np.arange(GRID_W)[None, :]
    cs = np.clip(qc - WIN_C // 2, 0, GRID_W - WIN_C)
    valid = (kc >= cs) & (kc < cs + WIN_C)
    side = GRID_W - WIN_C
    rp = jnp.pad(rpb.astype(F32) * LOG2E, ((0, 0), (0, 3), (side, side)))
    t1 = jnp.stack([rp[:, :, GRID_W - 1 - q:2 * GRID_W - 1 - q] for q in range(GRID_W)], axis=2)
    row_ok = (np.arange(2 * WIN_R + 2) < 2 * WIN_R - 1)[None, :, None, None]
    t1 = jnp.where(jnp.asarray(valid[None, None] & row_ok), t1, NEG_INF)
    t2 = jnp.stack([jnp.stack([t1[:, par + jj:par + jj + 16:2] for jj in (0, 1)], axis=3) for par in (0, 1)],
                   axis=1)
    t2 = t2.reshape(NA_HEADS // 2, 2, 2, 8, GRID_W, 2, GRID_W)
    t2 = jnp.transpose(t2, (0, 2, 3, 1, 4, 5, 6))
    return t2.reshape(NA_HEADS // 2, 16, 2 * GRID_W, 2 * GRID_W)


def _ssd_kernel(xs_ref, bm_ref, cm_ref, z_ref, cwx_ref, cbx_ref, cwb_ref, cbb_ref, cwc_ref, cbc_ref,
                row_ref, d_ref, nrm_ref, o_ref,
                conv_in_s, conv_out_s, xm_s, bm_s, cm_s, bmt_s, g_s, y_s):
    t = xs_ref.shape[1]
    nc = t // CHUNK
    n = SSD_STATE
    half = SSD_CONV // 2
    n_pairs = HEADS_PER_GROUP // 2

    seg = conv_in_s.shape[1] // 8 - 2
    slabs = [(xs_ref, 0, cwx_ref, cbx_ref), (xs_ref, LANES, cwx_ref, cbx_ref),
             (bm_ref, 0, cwb_ref, cbb_ref), (cm_ref, 0, cwc_ref, cbc_ref)]
    taps, biases = [], []
    for s, (src_ref, l0, w_ref, b_ref) in enumerate(slabs):
        conv_in_s[s, 0:8, :] = jnp.zeros((8, LANES), F32)
        conv_in_s[s, 8 + t:, :] = jnp.zeros((conv_in_s.shape[1] - 8 - t, LANES), F32)
        conv_in_s[s, 8:8 + t, :] = src_ref[0, :, l0:l0 + LANES].astype(F32)
        taps.append([jnp.broadcast_to(0.5 * w_ref[k:k + 1, l0:l0 + LANES], (8, LANES)) for k in range(SSD_CONV)])
        biases.append(jnp.broadcast_to(0.5 * b_ref[:, l0:l0 + LANES], (8, LANES)))

    def conv_body(j, carry):
        for s in range(len(slabs)):
            acc = biases[s]
            for k in range(SSD_CONV):
                acc = acc + taps[s][k] * conv_in_s[s, pl.ds(8 - half + k + j, 8, stride=seg), :]
            conv_out_s[s, pl.ds(j, 8, stride=seg), :] = acc + acc * jnp.tanh(acc)
        return carry

    lax.fori_loop(0, seg, conv_body, 0, unroll=20)

    lane3 = lax.broadcasted_iota(jnp.int32, (1, 1, LANES), 2)
    mask3_a = (lane3 < SSD_HEAD_DIM).astype(F32).astype(BF16)
    mask3_b = (lane3 >= SSD_HEAD_DIM).astype(F32).astype(BF16)
    for p in range(n_pairs):
        xp = conv_out_s[p, 0:t, :].reshape(nc, CHUNK, LANES)
        y_s[:, :, p * LANES:(p + 1) * LANES] = xp * d_ref[:, p * LANES:(p + 1) * LANES]
        xb = xp.astype(BF16)
        xm_s[:, p, 0:CHUNK, :] = xb * mask3_a
        xm_s[:, p, CHUNK:2 * CHUNK, :] = xb * mask3_b
    bm_s[...] = conv_out_s[2, 0:t, :].astype(BF16).reshape(nc, CHUNK, n)
    cm_s[...] = conv_out_s[3, 0:t, :].astype(BF16).reshape(nc, CHUNK, n)

    ri = lax.broadcasted_iota(jnp.int32, (n, n), 0)
    ci = lax.broadcasted_iota(jnp.int32, (n, n), 1)
    eye = (ri == ci).astype(F32).astype(BF16)
    for c in range(nc):
        bmt_s[c] = lax.dot_general(eye, bm_s[c], NT_DIMS, preferred_element_type=F32).astype(BF16)

    lower = ci <= ri
    upper = ci >= ri
    lo = lax.broadcasted_iota(jnp.int32, (CHUNK, LANES), 1) < SSD_HEAD_DIM

    for c in range(nc):
        g_s[c] = lax.dot_general(cm_s[c], bm_s[c], NT_DIMS, preferred_element_type=F32)

    def chunk_step(c, direction, h, g):
        mask = lower if direction == 0 else upper
        off = direction * HEADS_PER_GROUP
        bt = bmt_s[c]
        ch = jnp.dot(cm_s[c], h.astype(BF16), preferred_element_type=F32)
        rows = row_ref[0, 0, c]
        edge = CHUNK - 1 if direction == 0 else 0
        dstate, decay_h = [], []
        for p in range(n_pairs):
            m_parts, a_cols, b_parts = [], [], []
            for hh in (2 * p, 2 * p + 1):
                k = off + hh
                a_col = jnp.broadcast_to(rows[ROW_A2 + k:ROW_A2 + k + 1, :], (CHUNK, CHUNK)).T
                a_row = rows[ROW_A + k:ROW_A + k + 1, :]
                w_row = rows[ROW_W + k:ROW_W + k + 1, :]
                decay = jnp.exp2(jnp.where(mask, a_col - a_row, -jnp.inf))
                m_parts.append((g * decay).astype(BF16))
                a_cols.append(a_col)
                b_parts.append(bt * w_row.astype(BF16))
            xm = xm_s[c, p]
            y_in = jnp.dot(jnp.concatenate(m_parts, axis=1), xm, preferred_element_type=F32)
            e_pair = jnp.exp2(jnp.where(lo, a_cols[0], a_cols[1]))
            y_off = ch[:, p * LANES:(p + 1) * LANES] * e_pair
            y_s[c, :, p * LANES:(p + 1) * LANES] += y_in + y_off
            dstate.append(jnp.dot(jnp.concatenate(b_parts, axis=1), xm, preferred_element_type=F32))
            decay_h.append(e_pair[edge:edge + 1, :])
        return h * jnp.concatenate(decay_h, axis=1) + jnp.concatenate(dstate, axis=1)

    def body(it, carry):
        h_f, h_b = carry
        return chunk_step(it, 0, h_f, g_s[it]), chunk_step(nc - 1 - it, 1, h_b, g_s[nc - 1 - it])

    h0 = jnp.zeros((n, GROUP_W), F32)
    lax.fori_loop(0, nc, body, (h0, h0), unroll=True)

    y = y_s[...].reshape(t, GROUP_W) * z_ref[0].astype(F32)
    o_ref[0] = _rms(y, nrm_ref[...]).astype(BF16)


def _ssd(proj, conv_w, conv_b, rowpack, d_rep, ssd_norm, b, t):
    nc = t // CHUNK
    xs_tile0 = T_XS * PROJ_TILE // GROUP_W
    z_tile0 = T_Z * PROJ_TILE // GROUP_W
    bm_tile0 = T_BM * PROJ_TILE // SSD_STATE
    cm_tile0 = T_CM * PROJ_TILE // SSD_STATE
    cb0 = D_INNER // SSD_STATE
    cc0 = cb0 + SSD_GROUPS
    seg = t // 8 + (4 - t // 8) % 8
    return pl.pallas_call(
        _ssd_kernel,
        grid=(b, SSD_GROUPS),
        in_specs=[
            pl.BlockSpec((1, t, GROUP_W), lambda i, g: (i, 0, xs_tile0 + g)),
            pl.BlockSpec((1, t, SSD_STATE), lambda i, g: (i, 0, bm_tile0 + g)),
            pl.BlockSpec((1, t, SSD_STATE), lambda i, g: (i, 0, cm_tile0 + g)),
            pl.BlockSpec((1, t, GROUP_W), lambda i, g: (i, 0, z_tile0 + g)),
            pl.BlockSpec((SSD_CONV, GROUP_W), lambda i, g: (0, g)),
            pl.BlockSpec((1, GROUP_W), lambda i, g: (0, g)),
            pl.BlockSpec((SSD_CONV, SSD_STATE), lambda i, g: (0, cb0 + g)),
            pl.BlockSpec((1, SSD_STATE), lambda i, g: (0, cb0 + g)),
            pl.BlockSpec((SSD_CONV, SSD_STATE), lambda i, g: (0, cc0 + g)),
            pl.BlockSpec((1, SSD_STATE), lambda i, g: (0, cc0 + g)),
            pl.BlockSpec((1, 1, nc, ROW_PACK, CHUNK), lambda i, g: (i, g, 0, 0, 0)),
            pl.BlockSpec((1, GROUP_W), lambda i, g: (0, g)),
            pl.BlockSpec((1, GROUP_W), lambda i, g: (0, g)),
        ],
        out_specs=pl.BlockSpec((1, t, GROUP_W), lambda i, g: (i, 0, g)),
        out_shape=jax.ShapeDtypeStruct((b, t, D_INNER), BF16),
        scratch_shapes=[
            pltpu.VMEM((4, 8 * (seg + 2), LANES), F32),
            pltpu.VMEM((4, 8 * seg, LANES), F32),
            pltpu.VMEM((nc, HEADS_PER_GROUP // 2, 2 * CHUNK, LANES), BF16),
            pltpu.VMEM((nc, CHUNK, SSD_STATE), BF16),
            pltpu.VMEM((nc, CHUNK, SSD_STATE), BF16),
            pltpu.VMEM((nc, SSD_STATE, CHUNK), BF16),
            pltpu.VMEM((nc, CHUNK, CHUNK), F32),
            pltpu.VMEM((nc, CHUNK, GROUP_W), F32),
        ],
        compiler_params=_params(("parallel", "arbitrary")),
        name="ssd",
    )(proj, proj, proj, proj, conv_w, conv_b, conv_w, conv_b, conv_w, conv_b,
      rowpack, d_rep, ssd_norm)


def _mem_kernel(q_ref, mem_ref, gm_ref, wkv_ref, gk_ref, o_ref, k_s, v_s):
    width = MEM_HEADS * MEM_HEAD_DIM

    @pl.when(pl.program_id(1) == 0)
    def _():
        mn = _rms(mem_ref[0], gm_ref[...]).astype(BF16)
        kv = jnp.dot(mn, wkv_ref[...], preferred_element_type=F32)
        for h in range(MEM_HEADS):
            hs = slice(h * MEM_HEAD_DIM, (h + 1) * MEM_HEAD_DIM)
            k_s[:, hs] = _rms(kv[:, hs], gk_ref[...]).astype(BF16)
        v_s[...] = kv[:, width:].astype(BF16)

    for h in range(MEM_HEADS):
        hs = slice(h * MEM_HEAD_DIM, (h + 1) * MEM_HEAD_DIM)
        s = lax.dot_general(q_ref[0, :, hs], k_s[:, hs], NT_DIMS, preferred_element_type=F32)
        m = jnp.max(s, axis=1, keepdims=True)
        e = jnp.exp2(s - m)
        l = jnp.sum(e, axis=1, keepdims=True)
        o = jnp.dot(e.astype(BF16), v_s[:, hs], preferred_element_type=F32) * (1.0 / l)
        o_ref[0, :, hs] = o.astype(BF16)


def _mem(proj, mem, g_mem, w_kv, g_k, b, t, tq=2048):
    width = MEM_HEADS * MEM_HEAD_DIM
    return pl.pallas_call(
        _mem_kernel,
        grid=(b, t // tq),
        in_specs=[
            pl.BlockSpec((1, tq, PROJ_TILE), lambda i, j: (i, j, T_QM)),
            pl.BlockSpec((1, MEM_TOKENS, D_MODEL), lambda i, j: (i, 0, 0)),
            _const_spec((1, D_MODEL)),
            _const_spec((D_MODEL, 2 * width)),
            _const_spec((1, MEM_HEAD_DIM)),
        ],
        out_specs=pl.BlockSpec((1, tq, width), lambda i, j: (i, j, 0)),
        out_shape=jax.ShapeDtypeStruct((b, t, width), BF16),
        scratch_shapes=[pltpu.VMEM((MEM_TOKENS, width), BF16), pltpu.VMEM((MEM_TOKENS, width), BF16)],
        compiler_params=_params(("parallel", "arbitrary")),
        name="mem_attn",
    )(proj, mem, g_mem, w_kv, g_k)


def _merge_kernel(x_ref, ona_ref, ossd_ref, omem_ref, gna_ref, gssd_ref, gmem_ref,
                  wna_ref, wssd_ref, wmem_ref, wout_ref, o_ref):
    merged = gna_ref[...].astype(F32) * jnp.dot(ona_ref[...], wna_ref[...], preferred_element_type=F32)
    merged += gssd_ref[...].astype(F32) * jnp.dot(ossd_ref[...], wssd_ref[...], preferred_element_type=F32)
    merged += gmem_ref[...].astype(F32) * jnp.dot(omem_ref[...], wmem_ref[...], preferred_element_type=F32)
    o_ref[...] = x_ref[...] + jnp.dot(merged.astype(BF16), wout_ref[...], preferred_element_type=F32)


def _merge(x1, o_na, o_ssd, o_mem, proj, w_na, w_ssd, w_mem, w_out, tm=512):
    n = x1.shape[0]
    tok = lambda w: pl.BlockSpec((tm, w), lambda i: (i, 0))
    gate = lambda k: pl.BlockSpec((tm, PROJ_TILE), lambda i: (i, T_GATE + k))
    return pl.pallas_call(
        _merge_kernel,
        grid=(n // tm,),
        in_specs=[tok(D_MODEL), tok(D_MODEL), tok(D_INNER), tok(D_MODEL), gate(0), gate(1), gate(2),
                  _const_spec((D_MODEL, D_MODEL)), _const_spec((D_INNER, D_MODEL)),
                  _const_spec((D_MODEL, D_MODEL)), _const_spec((D_MODEL, D_MODEL))],
        out_specs=tok(D_MODEL),
        out_shape=jax.ShapeDtypeStruct((n, D_MODEL), F32),
        compiler_params=_params(("parallel",)),
        name="merge_out",
    )(x1, o_na, o_ssd, o_mem, proj, proj, proj, w_na, w_ssd, w_mem, w_out)


def _prepare_weights(ffn1_norm, ffn1_w_gate, ffn1_w_up, ffn1_w_down, mix_norm, w_in, na_q_norm, na_k_norm,
                     na_rpb, conv_w, conv_b, dt_bias_f, dt_bias_b, a_log_f, a_log_b, ssd_d, ssd_norm,
                     mem_norm, w_mem_kv, mem_q_norm, mem_k_norm, w_br_na, w_br_ssd, w_br_mem, w_out,
                     ffn2_norm, ffn2_w_gate, ffn2_w_up, ffn2_w_down):
    bf = lambda w: w.astype(BF16)
    row = lambda v: v.astype(F32).reshape(1, -1)
    dt0 = 3 * NA_HEADS * NA_HEAD_DIM + MEM_HEADS * MEM_HEAD_DIM + D_INNER + D_INNER + 2 * SSD_GROUPS * SSD_STATE
    n_dt = 2 * SSD_HEADS
    w_main = bf(jnp.concatenate([w_in[:, :dt0], w_in[:, dt0 + n_dt:]], axis=1))
    perm = np.concatenate([np.concatenate([np.arange(g * HEADS_PER_GROUP, (g + 1) * HEADS_PER_GROUP),
                                           SSD_HEADS + np.arange(g * HEADS_PER_GROUP, (g + 1) * HEADS_PER_GROUP)])
                           for g in range(SSD_GROUPS)])
    w_dt = w_in[:, dt0:dt0 + n_dt][:, perm]
    wdt_r = bf(jnp.pad(w_dt, ((0, 0), (0, LANES - n_dt)))).T
    pad_v = lambda v: jnp.pad(v.astype(F32)[perm], (0, LANES - n_dt)).reshape(1, LANES)
    dt_bias = pad_v(jnp.concatenate([dt_bias_f, dt_bias_b]))
    a_neg = pad_v(jnp.concatenate([-jnp.exp(a_log_f.astype(F32)), -jnp.exp(a_log_b.astype(F32))]))
    is_fwd = pad_v(jnp.concatenate([jnp.ones((SSD_HEADS,), F32), jnp.zeros((SSD_HEADS,), F32)]))
    gains = jnp.stack([
        jnp.tile(na_q_norm.astype(F32), NA_HEADS) * (NA_HEAD_DIM ** -0.5 * LOG2E),
        jnp.tile(na_k_norm.astype(F32), NA_HEADS),
        jnp.ones((PROJ_TILE,), F32),
        jnp.tile(mem_q_norm.astype(F32), MEM_HEADS) * (MEM_HEAD_DIM ** -0.5 * LOG2E),
    ]).reshape(4, 1, PROJ_TILE)
    blk = np.arange(256) // NA_HEAD_DIM
    nmat64 = (blk[:, None] == blk[None, :]).astype(np.float32) / NA_HEAD_DIM
    nmat256 = np.full((256, 256), 1.0 / MEM_HEAD_DIM, np.float32)
    nmats = jnp.asarray(np.stack([nmat64, nmat256]), BF16)
    return dict(
        ffn1=(row(ffn1_norm), bf(ffn1_w_gate), bf(ffn1_w_up), bf(ffn1_w_down), row(mix_norm)),
        ffn2=(row(ffn2_norm), bf(ffn2_w_gate), bf(ffn2_w_up), bf(ffn2_w_down), row(ffn2_norm)),
        inproj=(w_main, gains, nmats, wdt_r),
        dt=(dt_bias, a_neg, is_fwd),
        na_bias=_na_bias_table(na_rpb),
        conv=(conv_w.astype(F32), row(conv_b)),
        d_rep=jnp.repeat(ssd_d.astype(F32), SSD_HEAD_DIM).reshape(1, D_INNER),
        ssd_norm=row(ssd_norm),
        mem=(row(mem_norm), bf(w_mem_kv), row(mem_k_norm)),
        merge=(bf(w_br_na), bf(w_br_ssd), bf(w_br_mem), bf(w_out)),
    )


def _encoder_layer(x, mem, w):
    b, t, _ = x.shape
    n = b * t
    x1, u = _ffn(x.reshape(n, D_MODEL), *w["ffn1"], with_u=True)
    proj, dtr = _inproj(u, *w["inproj"])
    rowpack = _dtprep(dtr, *w["dt"], b, t)
    proj3 = proj.reshape(b, t, N_PROJ_TILES * PROJ_TILE)
    o_na = _na(proj3, w["na_bias"], b, t)
    o_ssd = _ssd(proj3, *w["conv"], rowpack, w["d_rep"], w["ssd_norm"], b, t)
    o_mem = _mem(proj3, mem, *w["mem"], b, t)
    x2 = _merge(x1, o_na.reshape(n, -1), o_ssd.reshape(n, -1), o_mem.reshape(n, -1), proj, *w["merge"])
    (y,) = _ffn(x2, *w["ffn2"], with_u=False)
    return y.reshape(b, t, D_MODEL)


def kernel(x_prompt, x_sample, mem_prompt, mem_sample, ffn1_norm, ffn1_w_gate, ffn1_w_up, ffn1_w_down, mix_norm, w_in, na_q_norm, na_k_norm, na_rpb, conv_w, conv_b, dt_bias_f, dt_bias_b, a_log_f, a_log_b, ssd_d, ssd_norm, mem_norm, w_mem_kv, mem_q_norm, mem_k_norm, w_br_na, w_br_ssd, w_br_mem, w_out, ffn2_norm, ffn2_w_gate, ffn2_w_up, ffn2_w_down):
    layer = (ffn1_norm, ffn1_w_gate, ffn1_w_up, ffn1_w_down, mix_norm, w_in, na_q_norm, na_k_norm, na_rpb,
             conv_w, conv_b, dt_bias_f, dt_bias_b, a_log_f, a_log_b, ssd_d, ssd_norm, mem_norm, w_mem_kv,
             mem_q_norm, mem_k_norm, w_br_na, w_br_ssd, w_br_mem, w_out, ffn2_norm, ffn2_w_gate, ffn2_w_up,
             ffn2_w_down)
    assert all(p.shape[0] == 1 for p in layer), "single-layer model"
    w = _prepare_weights(*[p[0] for p in layer])
    return (_encoder_layer(x_prompt, mem_prompt, w), _encoder_layer(x_sample, mem_sample, w))
```
